```python
import math
import jax, jax.numpy as jnp
from jax import lax
import numpy as np

D_MODEL = 1024
BATCH = 16
SEQ = 4096
DEPTH = 4

N_A_LAYERS = DEPTH // 2
N_B_LAYERS = DEPTH - N_A_LAYERS
D_FF = 2816
SSM_WIDTH = D_MODEL
SSM_GROUP = 16
SSM_GROUPS = SSM_WIDTH // SSM_GROUP
SSM_STATE = 64
DT_MIN = 1e-3
DT_MAX = 1e-1
SB_HEADS = 16
SB_HEAD_DIM = D_MODEL // SB_HEADS
SB_WIDTH = SB_HEADS * SB_HEAD_DIM
Q_BLOCK = 128
NORM_EPS = 1e-6
N_NORMS = 6

kernel_name = "s5_yoco_stickbreaking_macaron_trunk"


def rms_norm(x, g):
    xf = x.astype(jnp.float32)
    y = xf * lax.rsqrt(jnp.mean(xf * xf, axis=-1, keepdims=True) + NORM_EPS)
    return (y * g.astype(jnp.float32)).astype(x.dtype)


def swiglu(h, w_gate, w_up, w_down):
    return (jax.nn.silu(h @ w_gate) * (h @ w_up)) @ w_down


def _cmul(ar, ai, br, bi):
    return ar * br - ai * bi, ar * bi + ai * br


def _scan_combine(e_prev, e_next):
    a1r, a1i, b1r, b1i = e_prev
    a2r, a2i, b2r, b2i = e_next
    ar, ai = _cmul(a2r, a2i, a1r, a1i)
    br, bi = _cmul(a2r, a2i, b1r, b1i)
    return ar, ai, br + b2r, bi + b2i


def s5_mixer(h, w_in, lam_re, lam_im, log_dt, b_re, b_im, c_re, c_im, d_skip, w_glu, w_out):
    f32 = jnp.float32
    bsz, L, _ = h.shape
    u = (h @ w_in).astype(f32).reshape(bsz, L, SSM_GROUPS, SSM_GROUP)
    dt = jnp.exp(log_dt.astype(f32))[:, None]
    lr, li = lam_re.astype(f32), lam_im.astype(f32)
    mag = jnp.exp(lr * dt)
    ab_re, ab_im = mag * jnp.cos(li * dt), mag * jnp.sin(li * dt)
    den = lr * lr + li * li
    nr, ni = ab_re - 1.0, ab_im
    f_re = (nr * lr + ni * li) / den
    f_im = (ni * lr - nr * li) / den
    bb_re, bb_im = _cmul(f_re[..., None], f_im[..., None], b_re.astype(f32), b_im.astype(f32))
    cr, ci = c_re.astype(f32), c_im.astype(f32)
    a_re_seq = jnp.broadcast_to(ab_re, (L, SSM_GROUPS, SSM_STATE))
    a_im_seq = jnp.broadcast_to(ab_im, (L, SSM_GROUPS, SSM_STATE))

    def one_sequence(u_seq):
        bu_re = jnp.einsum('lgc,gnc->lgn', u_seq, bb_re)
        bu_im = jnp.einsum('lgc,gnc->lgn', u_seq, bb_im)
        _, _, s_re, s_im = lax.associative_scan(
            _scan_combine, (a_re_seq, a_im_seq, bu_re, bu_im), axis=0)
        return jnp.einsum('lgn,gcn->lgc', s_re, cr) - jnp.einsum('lgn,gcn->lgc', s_im, ci)

    y = lax.map(one_sequence, u)
    y = y + d_skip.astype(f32) * u
    y = jax.nn.gelu(y.reshape(bsz, L, SSM_WIDTH)).astype(h.dtype)
    z = y * jax.nn.sigmoid(y @ w_glu)
    return z @ w_out


def stick_breaking_attention(q, k, v):
    f32 = jnp.float32
    L = q.shape[2]
    scale = 1.0 / math.sqrt(SB_HEAD_DIM)
    outs = []
    for blk in range(L // Q_BLOCK):
        q0 = blk * Q_BLOCK
        kl = q0 + Q_BLOCK
        qb = q[:, :, q0:kl].astype(f32)
        kb = k[:, :, :kl].astype(f32)
        vb = v[:, :, :kl].astype(f32)
        z = jnp.einsum('bhtd,bhsd->bhts', qb, kb) * scale
        t_idx = q0 + jnp.arange(Q_BLOCK)[:, None]
        s_idx = jnp.arange(kl)[None, :]
        causal = s_idx < t_idx
        log_keep = jnp.where(causal, jax.nn.log_sigmoid(-z), 0.0)
        after = lax.cumsum(log_keep, axis=3, reverse=True) - log_keep
        w = jnp.where(causal, jnp.exp(jax.nn.log_sigmoid(z) + after), 0.0)
        outs.append(jnp.einsum('bhts,bhsd->bhtd', w, vb))
    return jnp.concatenate(outs, axis=2).astype(q.dtype)


def _split_heads(t):
    bsz, L, _ = t.shape
    return t.reshape(bsz, L, SB_HEADS, SB_HEAD_DIM).transpose(0, 2, 1, 3)


def _merge_heads(t):
    bsz, H, L, Dh = t.shape
    return t.transpose(0, 2, 1, 3).reshape(bsz, L, H * Dh)


def setup_inputs(seed: int = 0) -> dict:
    key = jax.random.key(seed)
    ks = iter(jax.random.split(key, 32))
    f32 = jnp.float32

    def nrm(shape, std):
        return jax.random.normal(next(ks), shape, f32) * std

    NA, NB, G, N, C = N_A_LAYERS, N_B_LAYERS, SSM_GROUPS, SSM_STATE, SSM_GROUP
    x = nrm((BATCH, SEQ, D_MODEL), 1.0)
    norm_g = 1.0 + nrm((DEPTH, N_NORMS, D_MODEL), 0.02)
    ffn_w_gate = nrm((DEPTH, 2, D_MODEL, D_FF), D_MODEL ** -0.5)
    ffn_w_up = nrm((DEPTH, 2, D_MODEL, D_FF), D_MODEL ** -0.5)
    ffn_w_down = nrm((DEPTH, 2, D_FF, D_MODEL), D_FF ** -0.5)
    ssm_w_in = nrm((NA, D_MODEL, SSM_WIDTH), D_MODEL ** -0.5)
    ssm_lam_re = -0.5 + nrm((NA, G, N), 0.01)
    ssm_lam_im = math.pi * jnp.arange(N, dtype=f32)[None, None, :] + nrm((NA, G, N), 0.01)
    ssm_log_dt = jax.random.uniform(next(ks), (NA, G), f32, math.log(DT_MIN), math.log(DT_MAX))
    ssm_b_re = nrm((NA, G, N, C), (2.0 * C) ** -0.5)
    ssm_b_im = nrm((NA, G, N, C), (2.0 * C) ** -0.5)
    ssm_c_re = nrm((NA, G, C, N), (2.0 * N) ** -0.5)
    ssm_c_im = nrm((NA, G, C, N), (2.0 * N) ** -0.5)
    ssm_d = nrm((NA, G, C), 1.0)
    ssm_w_glu = nrm((NA, SSM_WIDTH, SSM_WIDTH), SSM_WIDTH ** -0.5)
    ssm_w_out = nrm((NA, SSM_WIDTH, D_MODEL), SSM_WIDTH ** -0.5)
    kv_norm_g = 1.0 + nrm((D_MODEL,), 0.02)
    w_k = nrm((D_MODEL, SB_WIDTH), D_MODEL ** -0.5)
    w_v = nrm((D_MODEL, SB_WIDTH), D_MODEL ** -0.5)
    sb_w_q = nrm((NB, D_MODEL, SB_WIDTH), D_MODEL ** -0.5)
    sb_w_o = nrm((NB, SB_WIDTH, D_MODEL), SB_WIDTH ** -0.5)
    return {"x": x, "norm_g": norm_g, "ffn_w_gate": ffn_w_gate, "ffn_w_up": ffn_w_up,
            "ffn_w_down": ffn_w_down, "ssm_w_in": ssm_w_in, "ssm_lam_re": ssm_lam_re,
            "ssm_lam_im": ssm_lam_im, "ssm_log_dt": ssm_log_dt, "ssm_b_re": ssm_b_re,
            "ssm_b_im": ssm_b_im, "ssm_c_re": ssm_c_re, "ssm_c_im": ssm_c_im, "ssm_d": ssm_d,
            "ssm_w_glu": ssm_w_glu, "ssm_w_out": ssm_w_out, "kv_norm_g": kv_norm_g,
            "w_k": w_k, "w_v": w_v, "sb_w_q": sb_w_q, "sb_w_o": sb_w_o}


def reference(x, norm_g, ffn_w_gate, ffn_w_up, ffn_w_down, ssm_w_in, ssm_lam_re, ssm_lam_im,
              ssm_log_dt, ssm_b_re, ssm_b_im, ssm_c_re, ssm_c_im, ssm_d, ssm_w_glu, ssm_w_out,
              kv_norm_g, w_k, w_v, sb_w_q, sb_w_o):
    k_shared = None
    v_shared = None
    for layer in range(DEPTH):
        g = norm_g[layer]
        h = swiglu(rms_norm(x, g[0]), ffn_w_gate[layer, 0], ffn_w_up[layer, 0], ffn_w_down[layer, 0])
        x = x + 0.5 * rms_norm(h, g[1])
        hn = rms_norm(x, g[2])
        if layer < N_A_LAYERS:
            a = layer
            mix = s5_mixer(hn, ssm_w_in[a], ssm_lam_re[a], ssm_lam_im[a], ssm_log_dt[a],
                           ssm_b_re[a], ssm_b_im[a], ssm_c_re[a], ssm_c_im[a], ssm_d[a],
                           ssm_w_glu[a], ssm_w_out[a])
        else:
            if k_shared is None:
                kv_in = rms_norm(x, kv_norm_g)
                k_shared = _split_heads(kv_in @ w_k)
                v_shared = _split_heads(kv_in @ w_v)
            b = layer - N_A_LAYERS
            q = _split_heads(hn @ sb_w_q[b])
            mix = _merge_heads(stick_breaking_attention(q, k_shared, v_shared)) @ sb_w_o[b]
        x = x + rms_norm(mix.astype(x.dtype), g[3])
        h = swiglu(rms_norm(x, g[4]), ffn_w_gate[layer, 1], ffn_w_up[layer, 1], ffn_w_down[layer, 1])
        x = x + 0.5 * rms_norm(h, g[5])
    return x
```

```python
import functools
import math

import jax
import jax.numpy as jnp
from jax import lax
from jax.experimental import pallas as pl
from jax.experimental.pallas import tpu as pltpu

F32 = jnp.float32
BF16 = jnp.bfloat16

NORM_EPS = 1e-6
SSM_GROUP = 16
SSM_STATE = 64
SB_HEAD_DIM = 64
LANES = 128
HEAD_PAIR = LANES // SB_HEAD_DIM * SB_HEAD_DIM
CH_BLOCK = 256
GROUPS_PER_BLOCK = CH_BLOCK // SSM_GROUP
STATES_PER_BLOCK = GROUPS_PER_BLOCK * SSM_STATE
SCAN_COLS = 512
VMEM_LIMIT = 56 * 1024 * 1024

FFN_ROWS = 512
S5_STEPS = 16
PROJ_ROWS = 512
ATT_Q = 256
ATT_K = 256


def _rms(x, g):
    ms = jnp.mean(x * x, axis=-1, keepdims=True)
    return x * lax.rsqrt(ms + NORM_EPS) * g


def _dot(a, b):
    return jnp.dot(a, b, preferred_element_type=F32)


def _const_spec(shape):
    nd = len(shape)
    return pl.BlockSpec(shape, lambda *_: (0,) * nd, pipeline_mode=pl.Buffered(1))


def _ffn_kernel(x_ref, gi_ref, go_ref, wg_ref, wu_ref, wd_ref, o_ref):
    x = x_ref[...]
    h = _rms(x, gi_ref[...]).astype(BF16)
    g = _dot(h, wg_ref[...])
    u = _dot(h, wu_ref[...])
    a = (g * jax.nn.sigmoid(g) * u).astype(BF16)
    y = _dot(a, wd_ref[...])
    o_ref[...] = x + 0.5 * _rms(y, go_ref[...])


def _ffn(x, g_in, g_out, wg, wu, wd, *, batch, seq, src, dst):
    d = x.shape[-1]
    f = wg.shape[-1]
    tm = FFN_ROWS
    nl = seq // tm

    def spec(order):
        if order == "bl":
            return (batch * seq, d), pl.BlockSpec((tm, d), lambda b, l: (b * nl + l, 0))
        return (seq, batch * d), pl.BlockSpec((tm, d), lambda b, l: (l, b))

    if src == dst:
        n = (batch * seq) // tm
        grid = (n,)
        in_shape = out_shape = (batch * seq, d)
        x_spec = o_spec = pl.BlockSpec((tm, d), lambda i: (i, 0))
    else:
        grid = (batch, nl)
        in_shape, x_spec = spec(src)
        out_shape, o_spec = spec(dst)
    out = pl.pallas_call(
        _ffn_kernel,
        grid=grid,
        in_specs=[x_spec, _const_spec((1, d)), _const_spec((1, d)),
                  _const_spec((d, f)), _const_spec((d, f)), _const_spec((f, d))],
        out_specs=o_spec,
        out_shape=jax.ShapeDtypeStruct(out_shape, F32),
        compiler_params=pltpu.CompilerParams(
            dimension_semantics=("arbitrary",) * len(grid), vmem_limit_bytes=VMEM_LIMIT),
        name="ffn",
    )(x.reshape(in_shape), g_in, g_out, wg, wu, wd)
    return out.reshape(batch * seq, d)


def _s5_param_kernel(lr_ref, li_ref, ldt_ref, br_ref, bi_ref, ar_ref, ai_ref, bbr_ref, bbi_ref):
    lr = lr_ref[...]
    li = li_ref[...]
    dt = jnp.exp(ldt_ref[...])
    mag = jnp.exp(lr * dt)
    ab_re = mag * jnp.cos(li * dt)
    ab_im = mag * jnp.sin(li * dt)
    den = lr * lr + li * li
    nr = ab_re - 1.0
    ni = ab_im
    f_re = (nr * lr + ni * li) / den
    f_im = (ni * lr - nr * li) / den
    b_re = br_ref[...]
    b_im = bi_ref[...]
    ar_ref[...] = ab_re
    ai_ref[...] = ab_im
    bbr_ref[...] = f_re * b_re - f_im * b_im
    bbi_ref[...] = f_re * b_im + f_im * b_re


def _s5_params(lam_re, lam_im, log_dt, b_re, b_im, c_re, c_im):
    g, n = lam_re.shape
    c = b_re.shape[-1]
    rep = lambda t: jnp.repeat(t, c, axis=0)
    b_t = lambda t: jnp.transpose(t, (0, 2, 1)).reshape(g * c, n)
    shp = jax.ShapeDtypeStruct((g * c, n), F32)
    ab_re, ab_im, bb_re, bb_im = pl.pallas_call(
        _s5_param_kernel, out_shape=(shp,) * 4, name="s5_params",
    )(rep(lam_re), rep(lam_im), rep(log_dt[:, None]), b_t(b_re), b_t(b_im))
    a_re = ab_re[::c].reshape(1, g * n)
    a_im = ab_im[::c].reshape(1, g * n)
    nb = g // GROUPS_PER_BLOCK
    eye = jnp.eye(GROUPS_PER_BLOCK, dtype=F32)

    def b_block(t):
        t = t.reshape(nb, GROUPS_PER_BLOCK, c, n)
        return jnp.einsum("jgcn,gh->jgchn", t, eye).reshape(nb, CH_BLOCK, STATES_PER_BLOCK)

    def c_block(t):
        t = t.reshape(nb, GROUPS_PER_BLOCK, c, n)
        return jnp.einsum("jgcn,gh->jhngc", t, eye).reshape(nb, STATES_PER_BLOCK, CH_BLOCK)

    b_mat = jnp.concatenate([b_block(bb_re), b_block(bb_im)], axis=2).astype(BF16)
    c_mat = jnp.concatenate([c_block(c_re), -c_block(c_im)], axis=1).astype(BF16)
    return a_re, a_im, b_mat, c_mat


def _gelu_tanh(y):
    return 0.5 * y * (1.0 + jnp.tanh(math.sqrt(2.0 / math.pi) * (y + 0.044715 * (y * y * y))))


def _s5_kernel(x_ref, gi_ref, go_ref, win_ref, bmat_ref, are_ref, aim_ref, cmat_ref, d_ref,
               wglu_ref, wout_ref, o_ref, sre_ref, sim_ref, bu_ref, sbf_ref, *, batch, steps):
    @pl.when(pl.program_id(0) == 0)
    def _():
        sre_ref[...] = jnp.zeros_like(sre_ref)
        sim_ref[...] = jnp.zeros_like(sim_ref)

    x = x_ref[...]
    hn = _rms(x, gi_ref[...]).astype(BF16)
    u = _dot(hn, win_ref[...])
    u_bf = u.astype(BF16)
    nb = bmat_ref.shape[0]
    spb = STATES_PER_BLOCK
    y_parts = []
    for j in range(nb):
        bu_ref[...] = _dot(u_bf[:, j * CH_BLOCK:(j + 1) * CH_BLOCK], bmat_ref[j])
        for c0 in range(0, spb, SCAN_COLS):
            cols = slice(j * spb + c0, j * spb + c0 + SCAN_COLS)
            a_re = jnp.broadcast_to(are_ref[:, cols], (batch, SCAN_COLS))
            a_im = jnp.broadcast_to(aim_ref[:, cols], (batch, SCAN_COLS))

            def step(t, carry, c0=c0, a_re=a_re, a_im=a_im):
                s_re, s_im = carry
                rows = pl.ds(pl.multiple_of(t * batch, batch), batch)
                n_re = a_re * s_re - a_im * s_im + bu_ref[rows, c0:c0 + SCAN_COLS]
                n_im = a_re * s_im + a_im * s_re + bu_ref[rows, spb + c0:spb + c0 + SCAN_COLS]
                sbf_ref[rows, c0:c0 + SCAN_COLS] = n_re.astype(BF16)
                sbf_ref[rows, spb + c0:spb + c0 + SCAN_COLS] = n_im.astype(BF16)
                return n_re, n_im

            s_re, s_im = lax.fori_loop(0, steps, step, (sre_ref[:, cols], sim_ref[:, cols]),
                                       unroll=True)
            sre_ref[:, cols] = s_re
            sim_ref[:, cols] = s_im
        y_parts.append(_dot(sbf_ref[...], cmat_ref[j]))
    y = jnp.concatenate(y_parts, axis=1) + d_ref[...] * u
    y = _gelu_tanh(y)
    z = y * jax.nn.sigmoid(_dot(y.astype(BF16), wglu_ref[...]))
    mix = _dot(z.astype(BF16), wout_ref[...])
    o_ref[...] = x + _rms(mix, go_ref[...])


def _s5(x, g_in, g_out, w_in, params, d_skip, w_glu, w_out, *, batch, seq):
    a_re, a_im, b_mat, c_mat = params
    d = x.shape[-1]
    rows = S5_STEPS * batch
    n_states = a_re.shape[-1]
    kern = functools.partial(_s5_kernel, batch=batch, steps=S5_STEPS)
    row_spec = pl.BlockSpec((rows, d), lambda i: (i, 0))
    return pl.pallas_call(
        kern,
        grid=(seq // S5_STEPS,),
        in_specs=[row_spec, _const_spec((1, d)), _const_spec((1, d)), _const_spec(w_in.shape),
                  _const_spec(b_mat.shape), _const_spec(a_re.shape), _const_spec(a_im.shape),
                  _const_spec(c_mat.shape), _const_spec((1, d)),
                  _const_spec(w_glu.shape), _const_spec(w_out.shape)],
        out_specs=row_spec,
        out_shape=jax.ShapeDtypeStruct(x.shape, F32),
        scratch_shapes=[pltpu.VMEM((batch, n_states), F32), pltpu.VMEM((batch, n_states), F32),
                        pltpu.VMEM((rows, 2 * STATES_PER_BLOCK), F32),
                        pltpu.VMEM((rows, 2 * STATES_PER_BLOCK), BF16)],
        compiler_params=pltpu.CompilerParams(
            dimension_semantics=("arbitrary",), vmem_limit_bytes=VMEM_LIMIT),
        name="s5_mixer",
    )(x, g_in, g_out, w_in, b_mat, a_re, a_im, c_mat, d_skip, w_glu, w_out)


def _proj_kernel(*refs, n_norm, norm_of, scales):
    x_ref = refs[0]
    g_refs = refs[1:1 + n_norm]
    n_out = len(norm_of)
    w_refs = refs[1 + n_norm:1 + n_norm + n_out]
    o_refs = refs[1 + n_norm + n_out:]
    x = x_ref[...]
    hs = [_rms(x, g[...]).astype(BF16) for g in g_refs]
    for w_ref, o_ref, gi, sc in zip(w_refs, o_refs, norm_of, scales):
        y = _dot(hs[gi], w_ref[...])
        if sc != 1.0:
            y = y * sc
        o_ref[...] = y.astype(o_ref.dtype)


def _proj(x, gains, weights, norm_of, scales):
    t, d = x.shape
    tm = PROJ_ROWS
    kern = functools.partial(_proj_kernel, n_norm=len(gains), norm_of=tuple(norm_of),
                             scales=tuple(scales))
    row_spec = pl.BlockSpec((tm, d), lambda i: (i, 0))
    outs = pl.pallas_call(
        kern,
        grid=(t // tm,),
        in_specs=[row_spec] + [_const_spec((1, d))] * len(gains)
                 + [_const_spec(w.shape) for w in weights],
        out_specs=[pl.BlockSpec((tm, w.shape[1]), lambda i: (i, 0)) for w in weights],
        out_shape=[jax.ShapeDtypeStruct((t, w.shape[1]), BF16) for w in weights],
        compiler_params=pltpu.CompilerParams(
            dimension_semantics=("arbitrary",), vmem_limit_bytes=VMEM_LIMIT),
        name="qkv_proj",
    )(x, *gains, *weights)
    return outs


def _softplus(z):
    return jnp.maximum(z, 0.0) + jnp.log1p(jnp.exp(-jnp.abs(z)))


def _attn_kernel(q_ref, k_ref, v_ref, o_ref, acc_ref, c_ref):
    tq = q_ref.shape[0]
    tk = ATT_K
    qi = pl.program_id(2)
    q = q_ref[...]
    lane = lax.broadcasted_iota(jnp.int32, (1, LANES), 1)
    first_head = lane < SB_HEAD_DIM
    r = lax.broadcasted_iota(jnp.int32, (tk, tk), 0)
    c = lax.broadcasted_iota(jnp.int32, (tk, tk), 1)
    later = jnp.where(r > c, 1.0, 0.0).astype(BF16)
    causal = c < r

    def block(h, qh, kb, diag):
        ks = pl.ds(pl.multiple_of(kb * tk, tk), tk)
        k = k_ref[ks, :]
        v = v_ref[ks, :]
        z = lax.dot_general(qh, k, (((1,), (1,)), ((), ())), preferred_element_type=F32)
        sp = _softplus(z)
        log_keep = -sp
        if diag:
            log_keep = jnp.where(causal, log_keep, 0.0)
        hi = log_keep.astype(BF16)
        lo = (log_keep - hi.astype(F32)).astype(BF16)
        after = _dot(hi, later) + _dot(lo, later) + c_ref[h]
        w = jnp.exp(z - sp + after)
        if diag:
            w = jnp.where(causal, w, 0.0)
        acc_ref[h] += _dot(w.astype(BF16), v)
        c_ref[h] += jnp.sum(log_keep, axis=1, keepdims=True)

    acc_ref[...] = jnp.zeros_like(acc_ref)
    c_ref[...] = jnp.zeros_like(c_ref)
    for h in range(2):
        qh = jnp.where(first_head if h == 0 else jnp.logical_not(first_head), q, jnp.zeros_like(q))
        block(h, qh, qi, True)

        def body(i, carry, h=h, qh=qh):
            block(h, qh, qi - 1 - i, False)
            return carry

        lax.fori_loop(0, qi, body, 0)
    o_ref[...] = jnp.where(first_head, acc_ref[0], acc_ref[1]).astype(o_ref.dtype)


def _attention(q, k, v, *, batch, seq):
    t, width = q.shape
    assert ATT_Q == ATT_K
    nq = seq // ATT_Q
    n_pairs = width // HEAD_PAIR
    kv_spec = pl.BlockSpec((seq, HEAD_PAIR), lambda b, p, i: (b, p))
    q_spec = pl.BlockSpec((ATT_Q, HEAD_PAIR), lambda b, p, i: (b * nq + i, p))
    return pl.pallas_call(
        _attn_kernel,
        grid=(batch, n_pairs, nq),
        in_specs=[q_spec, kv_spec, kv_spec],
        out_specs=q_spec,
        out_shape=jax.ShapeDtypeStruct((t, width), BF16),
        scratch_shapes=[pltpu.VMEM((2, ATT_Q, HEAD_PAIR), F32), pltpu.VMEM((2, ATT_Q, 1), F32)],
        compiler_params=pltpu.CompilerParams(
            dimension_semantics=("arbitrary",) * 3, vmem_limit_bytes=VMEM_LIMIT),
        name="sb_attention",
    )(q, k, v)


def _oproj_kernel(a_ref, x_ref, w_ref, g_ref, o_ref):
    mix = _dot(a_ref[...], w_ref[...])
    o_ref[...] = x_ref[...] + _rms(mix, g_ref[...])


def _oproj(a, x, w, g):
    t, d = x.shape
    tm = PROJ_ROWS
    row = lambda width: pl.BlockSpec((tm, width), lambda i: (i, 0))
    return pl.pallas_call(
        _oproj_kernel,
        grid=(t // tm,),
        in_specs=[row(a.shape[1]), row(d), _const_spec(w.shape), _const_spec((1, d))],
        out_specs=row(d),
        out_shape=jax.ShapeDtypeStruct((t, d), F32),
        compiler_params=pltpu.CompilerParams(
            dimension_semantics=("arbitrary",), vmem_limit_bytes=VMEM_LIMIT),
        name="attn_out_proj",
    )(a, x, w, g)


def kernel(x, norm_g, ffn_w_gate, ffn_w_up, ffn_w_down, ssm_w_in, ssm_lam_re, ssm_lam_im, ssm_log_dt, ssm_b_re, ssm_b_im, ssm_c_re, ssm_c_im, ssm_d, ssm_w_glu, ssm_w_out, kv_norm_g, w_k, w_v, sb_w_q, sb_w_o):
    batch, seq, d = x.shape
    depth = norm_g.shape[0]
    n_a = ssm_w_in.shape[0]
    bf = lambda t: t.astype(BF16)
    gain = lambda layer, i: norm_g[layer, i][None, :]
    q_scale = 1.0 / math.sqrt(SB_HEAD_DIM)

    h = x.reshape(batch * seq, d)
    order = "bl"
    k_shared = v_shared = None
    for layer in range(depth):
        is_ssm = layer < n_a
        want = "lb" if is_ssm else "bl"
        h = _ffn(h, gain(layer, 0), gain(layer, 1), bf(ffn_w_gate[layer, 0]), bf(ffn_w_up[layer, 0]),
                 bf(ffn_w_down[layer, 0]), batch=batch, seq=seq, src=order, dst=want)
        order = want
        if is_ssm:
            a = layer
            params = _s5_params(ssm_lam_re[a], ssm_lam_im[a], ssm_log_dt[a], ssm_b_re[a], ssm_b_im[a],
                                ssm_c_re[a], ssm_c_im[a])
            h = _s5(h, gain(layer, 2), gain(layer, 3), bf(ssm_w_in[a]), params,
                    ssm_d[a].reshape(1, d), bf(ssm_w_glu[a]), bf(ssm_w_out[a]), batch=batch, seq=seq)
        else:
            b = layer - n_a
            if k_shared is None:
                q, k_shared, v_shared = _proj(
                    h, [gain(layer, 2), kv_norm_g[None, :]], [bf(sb_w_q[b]), bf(w_k), bf(w_v)],
                    norm_of=(0, 1, 1), scales=(q_scale, 1.0, 1.0))
            else:
                (q,) = _proj(h, [gain(layer, 2)], [bf(sb_w_q[b])], norm_of=(0,), scales=(q_scale,))
            att = _attention(q, k_shared, v_shared, batch=batch, seq=seq)
            h = _oproj(att, h, bf(sb_w_o[b]), gain(layer, 3))
        nxt = "lb" if (layer + 1 < n_a) else "bl"
        h = _ffn(h, gain(layer, 4), gain(layer, 5), bf(ffn_w_gate[layer, 1]), bf(ffn_w_up[layer, 1]),
                 bf(ffn_w_down[layer, 1]), batch=batch, seq=seq, src=order, dst=nxt)
        order = nxt
    return h.reshape(batch, seq, d)
```

```python
import functools
import math

import jax
import jax.numpy as jnp
from jax import lax
from jax.experimental import pallas as pl
from jax.experimental.pallas import tpu as pltpu

F32 = jnp.float32
BF16 = jnp.bfloat16

NORM_EPS = 1e-6
SSM_GROUP = 16
SSM_STATE = 64
SB_HEAD_DIM = 64
LANES = 128
HEAD_PAIR = LANES // SB_HEAD_DIM * SB_HEAD_DIM
CH_BLOCK = 256
GROUPS_PER_BLOCK = CH_BLOCK // SSM_GROUP
STATES_PER_BLOCK = GROUPS_PER_BLOCK * SSM_STATE
SCAN_COLS = 512
VMEM_LIMIT = 56 * 1024 * 1024

FFN_ROWS = 512
S5_STEPS = 16
PROJ_ROWS = 512
ATT_Q = 256
ATT_K = 256
EXP_UNDERFLOW = -104.0


def _rms(x, g):
    ms = jnp.mean(x * x, axis=-1, keepdims=True)
    return x * lax.rsqrt(ms + NORM_EPS) * g


def _dot(a, b):
    return jnp.dot(a, b, preferred_element_type=F32)


def _const_spec(shape):
    nd = len(shape)
    return pl.BlockSpec(shape, lambda *_: (0,) * nd, pipeline_mode=pl.Buffered(1))


def _ffn_kernel(x_ref, gi_ref, go_ref, wg_ref, wu_ref, wd_ref, o_ref):
    x = x_ref[...]
    h = _rms(x, gi_ref[...]).astype(BF16)
    g = _dot(h, wg_ref[...])
    u = _dot(h, wu_ref[...])
    a = (g * jax.nn.sigmoid(g) * u).astype(BF16)
    y = _dot(a, wd_ref[...])
    o_ref[...] = x + 0.5 * _rms(y, go_ref[...])


def _ffn(x, g_in, g_out, wg, wu, wd, *, batch, seq, src, dst):
    d = x.shape[-1]
    f = wg.shape[-1]
    tm = FFN_ROWS
    nl = seq // tm

    def spec(order):
        if order == "bl":
            return (batch * seq, d), pl.BlockSpec((tm, d), lambda b, l: (b * nl + l, 0))
        return (seq, batch * d), pl.BlockSpec((tm, d), lambda b, l: (l, b))

    if src == dst:
        n = (batch * seq) // tm
        grid = (n,)
        in_shape = out_shape = (batch * seq, d)
        x_spec = o_spec = pl.BlockSpec((tm, d), lambda i: (i, 0))
    else:
        grid = (batch, nl)
        in_shape, x_spec = spec(src)
        out_shape, o_spec = spec(dst)
    out = pl.pallas_call(
        _ffn_kernel,
        grid=grid,
        in_specs=[x_spec, _const_spec((1, d)), _const_spec((1, d)),
                  _const_spec((d, f)), _const_spec((d, f)), _const_spec((f, d))],
        out_specs=o_spec,
        out_shape=jax.ShapeDtypeStruct(out_shape, F32),
        compiler_params=pltpu.CompilerParams(
            dimension_semantics=("arbitrary",) * len(grid), vmem_limit_bytes=VMEM_LIMIT),
        name="ffn",
    )(x.reshape(in_shape), g_in, g_out, wg, wu, wd)
    return out.reshape(batch * seq, d)


def _s5_param_kernel(lr_ref, li_ref, ldt_ref, br_ref, bi_ref, ar_ref, ai_ref, bbr_ref, bbi_ref):
    lr = lr_ref[...]
    li = li_ref[...]
    dt = jnp.exp(ldt_ref[...])
    mag = jnp.exp(lr * dt)
    ab_re = mag * jnp.cos(li * dt)
    ab_im = mag * jnp.sin(li * dt)
    den = lr * lr + li * li
    nr = ab_re - 1.0
    ni = ab_im
    f_re = (nr * lr + ni * li) / den
    f_im = (ni * lr - nr * li) / den
    b_re = br_ref[...]
    b_im = bi_ref[...]
    ar_ref[...] = ab_re
    ai_ref[...] = ab_im
    bbr_ref[...] = f_re * b_re - f_im * b_im
    bbi_ref[...] = f_re * b_im + f_im * b_re


def _s5_params(lam_re, lam_im, log_dt, b_re, b_im, c_re, c_im):
    g, n = lam_re.shape
    c = b_re.shape[-1]
    rep = lambda t: jnp.repeat(t, c, axis=0)
    b_t = lambda t: jnp.transpose(t, (0, 2, 1)).reshape(g * c, n)
    shp = jax.ShapeDtypeStruct((g * c, n), F32)
    ab_re, ab_im, bb_re, bb_im = pl.pallas_call(
        _s5_param_kernel, out_shape=(shp,) * 4, name="s5_params",
    )(rep(lam_re), rep(lam_im), rep(log_dt[:, None]), b_t(b_re), b_t(b_im))
    a_re = ab_re[::c].reshape(1, g * n)
    a_im = ab_im[::c].reshape(1, g * n)
    nb = g // GROUPS_PER_BLOCK
    eye = jnp.eye(GROUPS_PER_BLOCK, dtype=F32)

    def b_block(t):
        t = t.reshape(nb, GROUPS_PER_BLOCK, c, n)
        return jnp.einsum("jgcn,gh->jgchn", t, eye).reshape(nb, CH_BLOCK, STATES_PER_BLOCK)

    def c_block(t):
        t = t.reshape(nb, GROUPS_PER_BLOCK, c, n)
        return jnp.einsum("jgcn,gh->jhngc", t, eye).reshape(nb, STATES_PER_BLOCK, CH_BLOCK)

    b_mat = jnp.concatenate([b_block(bb_re), b_block(bb_im)], axis=2).astype(BF16)
    c_mat = jnp.concatenate([c_block(c_re), -c_block(c_im)], axis=1).astype(BF16)
    return a_re, a_im, b_mat, c_mat


def _gelu_tanh(y):
    return 0.5 * y * (1.0 + jnp.tanh(math.sqrt(2.0 / math.pi) * (y + 0.044715 * (y * y * y))))


def _s5_kernel(x_ref, gi_ref, go_ref, win_ref, bmat_ref, are_ref, aim_ref, cmat_ref, d_ref,
               wglu_ref, wout_ref, o_ref, sre_ref, sim_ref, bu_ref, sbf_ref, *, batch, steps):
    @pl.when(pl.program_id(0) == 0)
    def _():
        sre_ref[...] = jnp.zeros_like(sre_ref)
        sim_ref[...] = jnp.zeros_like(sim_ref)

    x = x_ref[...]
    hn = _rms(x, gi_ref[...]).astype(BF16)
    u = _dot(hn, win_ref[...])
    u_bf = u.astype(BF16)
    nb = bmat_ref.shape[0]
    spb = STATES_PER_BLOCK
    y_parts = []
    for j in range(nb):
        bu_ref[...] = _dot(u_bf[:, j * CH_BLOCK:(j + 1) * CH_BLOCK], bmat_ref[j])
        for c0 in range(0, spb, SCAN_COLS):
            cols = slice(j * spb + c0, j * spb + c0 + SCAN_COLS)
            a_re = jnp.broadcast_to(are_ref[:, cols], (batch, SCAN_COLS))
            a_im = jnp.broadcast_to(aim_ref[:, cols], (batch, SCAN_COLS))

            def step(t, carry, c0=c0, a_re=a_re, a_im=a_im):
                s_re, s_im = carry
                rows = pl.ds(pl.multiple_of(t * batch, batch), batch)
                n_re = a_re * s_re - a_im * s_im + bu_ref[rows, c0:c0 + SCAN_COLS]
                n_im = a_re * s_im + a_im * s_re + bu_ref[rows, spb + c0:spb + c0 + SCAN_COLS]
                sbf_ref[rows, c0:c0 + SCAN_COLS] = n_re.astype(BF16)
                sbf_ref[rows, spb + c0:spb + c0 + SCAN_COLS] = n_im.astype(BF16)
                return n_re, n_im

            s_re, s_im = lax.fori_loop(0, steps, step, (sre_ref[:, cols], sim_ref[:, cols]),
                                       unroll=True)
            sre_ref[:, cols] = s_re
            sim_ref[:, cols] = s_im
        y_parts.append(_dot(sbf_ref[...], cmat_ref[j]))
    y = jnp.concatenate(y_parts, axis=1) + d_ref[...] * u
    y = _gelu_tanh(y)
    z = y * jax.nn.sigmoid(_dot(y.astype(BF16), wglu_ref[...]))
    mix = _dot(z.astype(BF16), wout_ref[...])
    o_ref[...] = x + _rms(mix, go_ref[...])


def _s5(x, g_in, g_out, w_in, params, d_skip, w_glu, w_out, *, batch, seq):
    a_re, a_im, b_mat, c_mat = params
    d = x.shape[-1]
    rows = S5_STEPS * batch
    n_states = a_re.shape[-1]
    kern = functools.partial(_s5_kernel, batch=batch, steps=S5_STEPS)
    row_spec = pl.BlockSpec((rows, d), lambda i: (i, 0))
    return pl.pallas_call(
        kern,
        grid=(seq // S5_STEPS,),
        in_specs=[row_spec, _const_spec((1, d)), _const_spec((1, d)), _const_spec(w_in.shape),
                  _const_spec(b_mat.shape), _const_spec(a_re.shape), _const_spec(a_im.shape),
                  _const_spec(c_mat.shape), _const_spec((1, d)),
                  _const_spec(w_glu.shape), _const_spec(w_out.shape)],
        out_specs=row_spec,
        out_shape=jax.ShapeDtypeStruct(x.shape, F32),
        scratch_shapes=[pltpu.VMEM((batch, n_states), F32), pltpu.VMEM((batch, n_states), F32),
                        pltpu.VMEM((rows, 2 * STATES_PER_BLOCK), F32),
                        pltpu.VMEM((rows, 2 * STATES_PER_BLOCK), BF16)],
        compiler_params=pltpu.CompilerParams(
            dimension_semantics=("arbitrary",), vmem_limit_bytes=VMEM_LIMIT),
        name="s5_mixer",
    )(x, g_in, g_out, w_in, b_mat, a_re, a_im, c_mat, d_skip, w_glu, w_out)


def _proj_kernel(*refs, n_norm, norm_of, scales):
    x_ref = refs[0]
    g_refs = refs[1:1 + n_norm]
    n_out = len(norm_of)
    w_refs = refs[1 + n_norm:1 + n_norm + n_out]
    o_refs = refs[1 + n_norm + n_out:]
    x = x_ref[...]
    hs = [_rms(x, g[...]).astype(BF16) for g in g_refs]
    for w_ref, o_ref, gi, sc in zip(w_refs, o_refs, norm_of, scales):
        y = _dot(hs[gi], w_ref[...])
        if sc != 1.0:
            y = y * sc
        o_ref[...] = y.astype(o_ref.dtype)


def _proj(x, gains, weights, norm_of, scales):
    t, d = x.shape
    tm = PROJ_ROWS
    kern = functools.partial(_proj_kernel, n_norm=len(gains), norm_of=tuple(norm_of),
                             scales=tuple(scales))
    row_spec = pl.BlockSpec((tm, d), lambda i: (i, 0))
    outs = pl.pallas_call(
        kern,
        grid=(t // tm,),
        in_specs=[row_spec] + [_const_spec((1, d))] * len(gains)
                 + [_const_spec(w.shape) for w in weights],
        out_specs=[pl.BlockSpec((tm, w.shape[1]), lambda i: (i, 0)) for w in weights],
        out_shape=[jax.ShapeDtypeStruct((t, w.shape[1]), BF16) for w in weights],
        compiler_params=pltpu.CompilerParams(
            dimension_semantics=("arbitrary",), vmem_limit_bytes=VMEM_LIMIT),
        name="qkv_proj",
    )(x, *gains, *weights)
    return outs


def _log_keep(z):
    return -jnp.maximum(z, 0.0) - jnp.log(1.0 + jnp.exp(-jnp.abs(z)))


def _attn_kernel(q_ref, k_ref, v_ref, o_ref, acc_ref, c_ref):
    tk = ATT_K
    qi = pl.program_id(2)
    q = q_ref[...]
    lane = lax.broadcasted_iota(jnp.int32, (1, LANES), 1)
    first_head = lane < SB_HEAD_DIM
    zero = jnp.zeros_like(q)
    q_heads = (jnp.where(first_head, q, zero), jnp.where(first_head, zero, q))
    r2 = lax.broadcasted_iota(jnp.int32, (2 * tk, tk), 0)
    c2 = lax.broadcasted_iota(jnp.int32, (2 * tk, tk), 1)
    r2 = jnp.where(r2 >= tk, r2 - tk, r2)
    not_before = jnp.where(r2 >= c2, 1.0, 0.0).astype(BF16)
    r = lax.broadcasted_iota(jnp.int32, (tk, tk), 0)
    c = lax.broadcasted_iota(jnp.int32, (tk, tk), 1)
    causal = c < r

    def block(kb, c_prev, diag):
        ks = pl.ds(pl.multiple_of(kb * tk, tk), tk)
        k = k_ref[ks, :]
        v = v_ref[ks, :]
        outs = []
        for h in range(2):
            z = lax.dot_general(q_heads[h], k, (((1,), (1,)), ((), ())), preferred_element_type=F32)
            log_keep = _log_keep(z)
            if diag:
                log_keep = jnp.where(causal, log_keep, 0.0)
            hi = log_keep.astype(BF16)
            lo = (log_keep - hi.astype(F32)).astype(BF16)
            incl = _dot(jnp.concatenate([hi, lo], axis=1), not_before)
            log_w = z + incl
            if c_prev is not None:
                log_w = log_w + c_prev[h]
            w = jnp.exp(log_w)
            if diag:
                w = jnp.where(causal, w, 0.0)
            outs.append((_dot(w.astype(BF16), v), jnp.sum(log_keep, axis=1, keepdims=True)))
        return outs

    @pl.when(qi == 0)
    def _():
        d = block(0, None, True)
        for h in range(2):
            acc_ref[h] = d[h][0]

    @pl.when(qi > 0)
    def _():
        d = block(qi, None, True)
        p = block(qi - 1, (d[0][1], d[1][1]), False)
        for h in range(2):
            acc_ref[h] = d[h][0] + p[h][0]
            c_ref[h] = d[h][1] + p[h][1]

        def cond(carry):
            kb, c_max = carry
            return jnp.logical_and(kb >= 0, c_max >= EXP_UNDERFLOW)

        def body(carry):
            kb, _ = carry
            o = block(kb, (c_ref[0], c_ref[1]), False)
            c_new = [c_ref[h] + o[h][1] for h in range(2)]
            for h in range(2):
                acc_ref[h] += o[h][0]
                c_ref[h] = c_new[h]
            return kb - 1, jnp.max(jnp.maximum(c_new[0], c_new[1]))

        lax.while_loop(cond, body, (qi - 2, jnp.max(c_ref[...])))

    o_ref[...] = jnp.where(first_head, acc_ref[0], acc_ref[1]).astype(o_ref.dtype)


def _attention(q, k, v, *, batch, seq):
    t, width = q.shape
    assert ATT_Q == ATT_K
    nq = seq // ATT_Q
    n_pairs = width // HEAD_PAIR
    kv_spec = pl.BlockSpec((seq, HEAD_PAIR), lambda b, p, i: (b, p))
    q_spec = pl.BlockSpec((ATT_Q, HEAD_PAIR), lambda b, p, i: (b * nq + i, p))
    return pl.pallas_call(
        _attn_kernel,
        grid=(batch, n_pairs, nq),
        in_specs=[q_spec, kv_spec, kv_spec],
        out_specs=q_spec,
        out_shape=jax.ShapeDtypeStruct((t, width), BF16),
        scratch_shapes=[pltpu.VMEM((2, ATT_Q, HEAD_PAIR), F32), pltpu.VMEM((2, ATT_Q, 1), F32)],
        compiler_params=pltpu.CompilerParams(
            dimension_semantics=("arbitrary",) * 3, vmem_limit_bytes=VMEM_LIMIT),
        name="sb_attention",
    )(q, k, v)


def _oproj_kernel(a_ref, x_ref, w_ref, g_ref, o_ref):
    mix = _dot(a_ref[...], w_ref[...])
    o_ref[...] = x_ref[...] + _rms(mix, g_ref[...])


def _oproj(a, x, w, g):
    t, d = x.shape
    tm = PROJ_ROWS
    row = lambda width: pl.BlockSpec((tm, width), lambda i: (i, 0))
    return pl.pallas_call(
        _oproj_kernel,
        grid=(t // tm,),
        in_specs=[row(a.shape[1]), row(d), _const_spec(w.shape), _const_spec((1, d))],
        out_specs=row(d),
        out_shape=jax.ShapeDtypeStruct((t, d), F32),
        compiler_params=pltpu.CompilerParams(
            dimension_semantics=("arbitrary",), vmem_limit_bytes=VMEM_LIMIT),
        name="attn_out_proj",
    )(a, x, w, g)


def kernel(x, norm_g, ffn_w_gate, ffn_w_up, ffn_w_down, ssm_w_in, ssm_lam_re, ssm_lam_im, ssm_log_dt, ssm_b_re, ssm_b_im, ssm_c_re, ssm_c_im, ssm_d, ssm_w_glu, ssm_w_out, kv_norm_g, w_k, w_v, sb_w_q, sb_w_o):
    batch, seq, d = x.shape
    depth = norm_g.shape[0]
    n_a = ssm_w_in.shape[0]
    bf = lambda t: t.astype(BF16)
    gain = lambda layer, i: norm_g[layer, i][None, :]
    q_scale = 1.0 / math.sqrt(SB_HEAD_DIM)

    h = x.reshape(batch * seq, d)
    order = "bl"
    k_shared = v_shared = None
    for layer in range(depth):
        is_ssm = layer < n_a
        want = "lb" if is_ssm else "bl"
        h = _ffn(h, gain(layer, 0), gain(layer, 1), bf(ffn_w_gate[layer, 0]), bf(ffn_w_up[layer, 0]),
                 bf(ffn_w_down[layer, 0]), batch=batch, seq=seq, src=order, dst=want)
        order = want
        if is_ssm:
            a = layer
            params = _s5_params(ssm_lam_re[a], ssm_lam_im[a], ssm_log_dt[a], ssm_b_re[a], ssm_b_im[a],
                                ssm_c_re[a], ssm_c_im[a])
            h = _s5(h, gain(layer, 2), gain(layer, 3), bf(ssm_w_in[a]), params,
                    ssm_d[a].reshape(1, d), bf(ssm_w_glu[a]), bf(ssm_w_out[a]), batch=batch, seq=seq)
        else:
            b = layer - n_a
            if k_shared is None:
                q, k_shared, v_shared = _proj(
                    h, [gain(layer, 2), kv_norm_g[None, :]], [bf(sb_w_q[b]), bf(w_k), bf(w_v)],
                    norm_of=(0, 1, 1), scales=(q_scale, 1.0, 1.0))
            else:
                (q,) = _proj(h, [gain(layer, 2)], [bf(sb_w_q[b])], norm_of=(0,), scales=(q_scale,))
            att = _attention(q, k_shared, v_shared, batch=batch, seq=seq)
            h = _oproj(att, h, bf(sb_w_o[b]), gain(layer, 3))
        nxt = "lb" if (layer + 1 < n_a) else "bl"
        h = _ffn(h, gain(layer, 4), gain(layer, 5), bf(ffn_w_gate[layer, 1]), bf(ffn_w_up[layer, 1]),
                 bf(ffn_w_down[layer, 1]), batch=batch, seq=seq, src=order, dst=nxt)
        order = nxt
    return h.reshape(batch, seq, d)
```

```python
import functools
import math

import jax
import jax.numpy as jnp
from jax import lax
from jax.experimental import pallas as pl
from jax.experimental.pallas import tpu as pltpu

F32 = jnp.float32
BF16 = jnp.bfloat16

NORM_EPS = 1e-6
SSM_GROUP = 16
SSM_STATE = 64
SB_HEAD_DIM = 64
LANES = 128
CH_BLOCK = 256
GROUPS_PER_BLOCK = CH_BLOCK // SSM_GROUP
STATES_PER_BLOCK = GROUPS_PER_BLOCK * SSM_STATE
SCAN_COLS = 512
VMEM_LIMIT = 56 * 1024 * 1024

FFN_ROWS = 512
S5_STEPS = 32
PROJ_ROWS = 512
ATT_Q = 256
ATT_K = 256
ATT_HEADS = 8
EXP_UNDERFLOW = 104.5
MASKED_SCORE = -1e30


def _rms(x, g):
    ms = jnp.mean(x * x, axis=-1, keepdims=True)
    return x * lax.rsqrt(ms + NORM_EPS) * g


def _dot(a, b):
    return jnp.dot(a, b, preferred_element_type=F32)


def _const_spec(shape):
    nd = len(shape)
    return pl.BlockSpec(shape, lambda *_: (0,) * nd, pipeline_mode=pl.Buffered(1))


def _ffn_kernel(*refs, attn_proj):
    if attn_proj:
        a_ref, wo_ref, gm_ref, x_ref, gi_ref, go_ref, wg_ref, wu_ref, wd_ref, o_ref = refs
        x = x_ref[...] + _rms(_dot(a_ref[...], wo_ref[...]), gm_ref[...])
    else:
        x_ref, gi_ref, go_ref, wg_ref, wu_ref, wd_ref, o_ref = refs
        x = x_ref[...]
    h = _rms(x, gi_ref[...]).astype(BF16)
    g = _dot(h, wg_ref[...])
    u = _dot(h, wu_ref[...])
    a = (g * jax.nn.sigmoid(g) * u).astype(BF16)
    y = _dot(a, wd_ref[...])
    o_ref[...] = x + 0.5 * _rms(y, go_ref[...])


def _ffn(x, g_in, g_out, wg, wu, wd, *, batch, seq, src, dst, attn=None):
    d = x.shape[-1]
    f = wg.shape[-1]
    tm = FFN_ROWS
    nl = seq // tm

    def spec(order):
        if order == "bl":
            return (batch * seq, d), pl.BlockSpec((tm, d), lambda b, l: (b * nl + l, 0))
        return (seq, batch * d), pl.BlockSpec((tm, d), lambda b, l: (l, b))

    if src == dst:
        n = (batch * seq) // tm
        grid = (n,)
        in_shape = out_shape = (batch * seq, d)
        x_spec = o_spec = pl.BlockSpec((tm, d), lambda i: (i, 0))
    else:
        grid = (batch, nl)
        in_shape, x_spec = spec(src)
        out_shape, o_spec = spec(dst)
    operands = [x.reshape(in_shape), g_in, g_out, wg, wu, wd]
    in_specs = [x_spec, _const_spec((1, d)), _const_spec((1, d)),
                _const_spec((d, f)), _const_spec((d, f)), _const_spec((f, d))]
    if attn is not None:
        assert src == dst
        a, w_o, g_mix = attn
        operands = [a, w_o, g_mix] + operands
        in_specs = [pl.BlockSpec((tm, a.shape[1]), lambda i: (i, 0)), _const_spec(w_o.shape),
                    _const_spec((1, d))] + in_specs
    out = pl.pallas_call(
        functools.partial(_ffn_kernel, attn_proj=attn is not None),
        grid=grid,
        in_specs=in_specs,
        out_specs=o_spec,
        out_shape=jax.ShapeDtypeStruct(out_shape, F32),
        compiler_params=pltpu.CompilerParams(
            dimension_semantics=("arbitrary",) * len(grid), vmem_limit_bytes=VMEM_LIMIT),
        name="ffn",
    )(*operands)
    return out.reshape(batch * seq, d)


def _s5_param_kernel(lr_ref, li_ref, ldt_ref, br_ref, bi_ref, ar_ref, ai_ref, bbr_ref, bbi_ref):
    lr = lr_ref[...]
    li = li_ref[...]
    dt = jnp.exp(ldt_ref[...])
    mag = jnp.exp(lr * dt)
    ab_re = mag * jnp.cos(li * dt)
    ab_im = mag * jnp.sin(li * dt)
    den = lr * lr + li * li
    nr = ab_re - 1.0
    ni = ab_im
    f_re = (nr * lr + ni * li) / den
    f_im = (ni * lr - nr * li) / den
    b_re = br_ref[...]
    b_im = bi_ref[...]
    ar_ref[...] = ab_re
    ai_ref[...] = ab_im
    bbr_ref[...] = f_re * b_re - f_im * b_im
    bbi_ref[...] = f_re * b_im + f_im * b_re


def _s5_params(lam_re, lam_im, log_dt, b_re, b_im, c_re, c_im):
    g, n = lam_re.shape
    c = b_re.shape[-1]
    rep = lambda t: jnp.repeat(t, c, axis=0)
    b_t = lambda t: jnp.transpose(t, (0, 2, 1)).reshape(g * c, n)
    shp = jax.ShapeDtypeStruct((g * c, n), F32)
    ab_re, ab_im, bb_re, bb_im = pl.pallas_call(
        _s5_param_kernel, out_shape=(shp,) * 4, name="s5_params",
    )(rep(lam_re), rep(lam_im), rep(log_dt[:, None]), b_t(b_re), b_t(b_im))
    a_re = ab_re[::c].reshape(1, g * n)
    a_im = ab_im[::c].reshape(1, g * n)
    nb = g // GROUPS_PER_BLOCK
    eye = jnp.eye(GROUPS_PER_BLOCK, dtype=F32)

    def b_block(t):
        t = t.reshape(nb, GROUPS_PER_BLOCK, c, n)
        return jnp.einsum("jgcn,gh->jgchn", t, eye).reshape(nb, CH_BLOCK, STATES_PER_BLOCK)

    def c_block(t):
        t = t.reshape(nb, GROUPS_PER_BLOCK, c, n)
        return jnp.einsum("jgcn,gh->jhngc", t, eye).reshape(nb, STATES_PER_BLOCK, CH_BLOCK)

    b_mat = jnp.concatenate([b_block(bb_re), b_block(bb_im)], axis=2).astype(BF16)
    c_mat = jnp.concatenate([c_block(c_re), -c_block(c_im)], axis=1).astype(BF16)
    return a_re, a_im, b_mat, c_mat


def _gelu_tanh(y):
    return 0.5 * y * (1.0 + jnp.tanh(math.sqrt(2.0 / math.pi) * (y + 0.044715 * (y * y * y))))


def _s5_kernel(x_ref, gi_ref, go_ref, win_ref, bmat_ref, are_ref, aim_ref, cmat_ref, d_ref,
               wglu_ref, wout_ref, o_ref, sre_ref, sim_ref, bu_ref, sbf_ref, *, batch, steps):
    @pl.when(pl.program_id(0) == 0)
    def _():
        sre_ref[...] = jnp.zeros_like(sre_ref)
        sim_ref[...] = jnp.zeros_like(sim_ref)

    x = x_ref[...]
    hn = _rms(x, gi_ref[...]).astype(BF16)
    u = _dot(hn, win_ref[...])
    u_bf = u.astype(BF16)
    nb = bmat_ref.shape[0]
    spb = STATES_PER_BLOCK
    y_parts = []
    for j in range(nb):
        bu_ref[...] = _dot(u_bf[:, j * CH_BLOCK:(j + 1) * CH_BLOCK], bmat_ref[j])
        for c0 in range(0, spb, SCAN_COLS):
            cols = slice(j * spb + c0, j * spb + c0 + SCAN_COLS)
            a_re = jnp.broadcast_to(are_ref[:, cols], (batch, SCAN_COLS))
            a_im = jnp.broadcast_to(aim_ref[:, cols], (batch, SCAN_COLS))

            def step(t, carry, c0=c0, a_re=a_re, a_im=a_im):
                s_re, s_im = carry
                rows = pl.ds(pl.multiple_of(t * batch, batch), batch)
                n_re = a_re * s_re - a_im * s_im + bu_ref[rows, c0:c0 + SCAN_COLS]
                n_im = a_re * s_im + a_im * s_re + bu_ref[rows, spb + c0:spb + c0 + SCAN_COLS]
                sbf_ref[rows, c0:c0 + SCAN_COLS] = n_re.astype(BF16)
                sbf_ref[rows, spb + c0:spb + c0 + SCAN_COLS] = n_im.astype(BF16)
                return n_re, n_im

            s_re, s_im = lax.fori_loop(0, steps, step, (sre_ref[:, cols], sim_ref[:, cols]),
                                       unroll=True)
            sre_ref[:, cols] = s_re
            sim_ref[:, cols] = s_im
        y_parts.append(_dot(sbf_ref[...], cmat_ref[j]))
    y = jnp.concatenate(y_parts, axis=1) + d_ref[...] * u
    y = _gelu_tanh(y)
    z = y * jax.nn.sigmoid(_dot(y.astype(BF16), wglu_ref[...]))
    mix = _dot(z.astype(BF16), wout_ref[...])
    o_ref[...] = x + _rms(mix, go_ref[...])


def _s5(x, g_in, g_out, w_in, params, d_skip, w_glu, w_out, *, batch, seq):
    a_re, a_im, b_mat, c_mat = params
    d = x.shape[-1]
    rows = S5_STEPS * batch
    n_states = a_re.shape[-1]
    kern = functools.partial(_s5_kernel, batch=batch, steps=S5_STEPS)
    row_spec = pl.BlockSpec((rows, d), lambda i: (i, 0))
    return pl.pallas_call(
        kern,
        grid=(seq // S5_STEPS,),
        in_specs=[row_spec, _const_spec((1, d)), _const_spec((1, d)), _const_spec(w_in.shape),
                  _const_spec(b_mat.shape), _const_spec(a_re.shape), _const_spec(a_im.shape),
                  _const_spec(c_mat.shape), _const_spec((1, d)),
                  _const_spec(w_glu.shape), _const_spec(w_out.shape)],
        out_specs=row_spec,
        out_shape=jax.ShapeDtypeStruct(x.shape, F32),
        scratch_shapes=[pltpu.VMEM((batch, n_states), F32), pltpu.VMEM((batch, n_states), F32),
                        pltpu.VMEM((rows, 2 * STATES_PER_BLOCK), F32),
                        pltpu.VMEM((rows, 2 * STATES_PER_BLOCK), BF16)],
        compiler_params=pltpu.CompilerParams(
            dimension_semantics=("arbitrary",), vmem_limit_bytes=VMEM_LIMIT),
        name="s5_mixer",
    )(x, g_in, g_out, w_in, b_mat, a_re, a_im, c_mat, d_skip, w_glu, w_out)


def _proj_kernel(*refs, n_norm, norm_of, scales):
    x_ref = refs[0]
    g_refs = refs[1:1 + n_norm]
    n_out = len(norm_of)
    w_refs = refs[1 + n_norm:1 + n_norm + n_out]
    o_refs = refs[1 + n_norm + n_out:]
    x = x_ref[...]
    hs = [_rms(x, g[...]).astype(BF16) for g in g_refs]
    for w_ref, o_ref, gi, sc in zip(w_refs, o_refs, norm_of, scales):
        y = _dot(hs[gi], w_ref[...])
        if sc != 1.0:
            y = y * sc
        o_ref[...] = y.astype(o_ref.dtype)


def _proj(x, gains, weights, norm_of, scales):
    t, d = x.shape
    tm = PROJ_ROWS
    kern = functools.partial(_proj_kernel, n_norm=len(gains), norm_of=tuple(norm_of),
                             scales=tuple(scales))
    row_spec = pl.BlockSpec((tm, d), lambda i: (i, 0))
    outs = pl.pallas_call(
        kern,
        grid=(t // tm,),
        in_specs=[row_spec] + [_const_spec((1, d))] * len(gains)
                 + [_const_spec(w.shape) for w in weights],
        out_specs=[pl.BlockSpec((tm, w.shape[1]), lambda i: (i, 0)) for w in weights],
        out_shape=[jax.ShapeDtypeStruct((t, w.shape[1]), BF16) for w in weights],
        compiler_params=pltpu.CompilerParams(
            dimension_semantics=("arbitrary",), vmem_limit_bytes=VMEM_LIMIT),
        name="qkv_proj",
    )(x, *gains, *weights)
    return outs


def _softplus(z):
    sign_bit = jnp.uint32(0x80000000)
    neg_abs = lax.bitcast_convert_type(lax.bitcast_convert_type(z, jnp.uint32) | sign_bit, F32)
    return jnp.maximum(z, 0.0) + jnp.log(1.0 + jnp.exp(neg_abs))


def _attn_kernel(q_ref, k_ref, v_ref, o_ref, acc_ref, c_ref):
    tk = ATT_K
    n_heads = q_ref.shape[1] // SB_HEAD_DIM
    qi = pl.program_id(2)
    lane = lax.broadcasted_iota(jnp.int32, (1, LANES), 1)
    first_head = lane < SB_HEAD_DIM
    q_heads = []
    for p in range(n_heads // 2):
        q = q_ref[:, p * LANES:(p + 1) * LANES]
        zero = jnp.zeros_like(q)
        q_heads += [jnp.where(first_head, q, zero), jnp.where(first_head, zero, q)]
    r = lax.broadcasted_iota(jnp.int32, (tk, tk), 0)
    c = lax.broadcasted_iota(jnp.int32, (tk, tk), 1)
    later = jnp.where(r > c, 1.0, 0.0).astype(BF16)
    causal = c < r

    def block(kb, c_prev, diag):
        ks = pl.ds(pl.multiple_of(kb * tk, tk), tk)
        outs = []
        for h in range(n_heads):
            pair = slice(h // 2 * LANES, (h // 2 + 1) * LANES)
            k = k_ref[ks, pair]
            v = v_ref[ks, pair]
            z = lax.dot_general(q_heads[h], k, (((1,), (1,)), ((), ())), preferred_element_type=F32)
            if diag:
                z = jnp.where(causal, z, MASKED_SCORE)
            drop = _softplus(z)
            incl = _dot(drop.astype(BF16), later) + drop
            log_w = z - incl
            if c_prev is not None:
                log_w = log_w - c_prev[h]
            w = jnp.exp(log_w)
            outs.append((_dot(w.astype(BF16), v), incl[:, 0:1]))
        return outs

    @pl.when(qi == 0)
    def _():
        d = block(0, None, True)
        for h in range(n_heads):
            acc_ref[h] = d[h][0]

    @pl.when(qi > 0)
    def _():
        d = block(qi, None, True)
        p = block(qi - 1, [d[h][1] for h in range(n_heads)], False)
        for h in range(n_heads):
            acc_ref[h] = d[h][0] + p[h][0]
            c_ref[h] = d[h][1] + p[h][1]

        def cond(carry):
            kb, c_min = carry
            return jnp.logical_and(kb >= 0, c_min <= EXP_UNDERFLOW)

        def body(carry):
            kb, _ = carry
            o = block(kb, [c_ref[h] for h in range(n_heads)], False)
            c_new = [c_ref[h] + o[h][1] for h in range(n_heads)]
            for h in range(n_heads):
                acc_ref[h] += o[h][0]
                c_ref[h] = c_new[h]
            return kb - 1, jnp.min(functools.reduce(jnp.minimum, c_new))

        lax.while_loop(cond, body, (qi - 2, jnp.min(c_ref[...])))

    for p in range(n_heads // 2):
        o_ref[:, p * LANES:(p + 1) * LANES] = jnp.where(
            first_head, acc_ref[2 * p], acc_ref[2 * p + 1]).astype(o_ref.dtype)


def _attention(q, k, v, *, batch, seq):
    t, width = q.shape
    assert ATT_Q == ATT_K
    nq = seq // ATT_Q
    lanes = ATT_HEADS * SB_HEAD_DIM
    kv_spec = pl.BlockSpec((seq, lanes), lambda b, p, i: (b, p))
    q_spec = pl.BlockSpec((ATT_Q, lanes), lambda b, p, i: (b * nq + i, p))
    return pl.pallas_call(
        _attn_kernel,
        grid=(batch, width // lanes, nq),
        in_specs=[q_spec, kv_spec, kv_spec],
        out_specs=q_spec,
        out_shape=jax.ShapeDtypeStruct((t, width), BF16),
        scratch_shapes=[pltpu.VMEM((ATT_HEADS, ATT_Q, LANES), F32), pltpu.VMEM((ATT_HEADS, ATT_Q, 1), F32)],
        compiler_params=pltpu.CompilerParams(
            dimension_semantics=("arbitrary",) * 3, vmem_limit_bytes=VMEM_LIMIT),
        name="sb_attention",
    )(q, k, v)


def kernel(x, norm_g, ffn_w_gate, ffn_w_up, ffn_w_down, ssm_w_in, ssm_lam_re, ssm_lam_im, ssm_log_dt, ssm_b_re, ssm_b_im, ssm_c_re, ssm_c_im, ssm_d, ssm_w_glu, ssm_w_out, kv_norm_g, w_k, w_v, sb_w_q, sb_w_o):
    batch, seq, d = x.shape
    depth = norm_g.shape[0]
    n_a = ssm_w_in.shape[0]
    bf = lambda t: t.astype(BF16)
    gain = lambda layer, i: norm_g[layer, i][None, :]
    q_scale = 1.0 / math.sqrt(SB_HEAD_DIM)

    h = x.reshape(batch * seq, d)
    order = "bl"
    k_shared = v_shared = None
    for layer in range(depth):
        is_ssm = layer < n_a
        attn = None
        want = "lb" if is_ssm else "bl"
        h = _ffn(h, gain(layer, 0), gain(layer, 1), bf(ffn_w_gate[layer, 0]), bf(ffn_w_up[layer, 0]),
                 bf(ffn_w_down[layer, 0]), batch=batch, seq=seq, src=order, dst=want)
        order = want
        if is_ssm:
            a = layer
            params = _s5_params(ssm_lam_re[a], ssm_lam_im[a], ssm_log_dt[a], ssm_b_re[a], ssm_b_im[a],
                                ssm_c_re[a], ssm_c_im[a])
            h = _s5(h, gain(layer, 2), gain(layer, 3), bf(ssm_w_in[a]), params,
                    ssm_d[a].reshape(1, d), bf(ssm_w_glu[a]), bf(ssm_w_out[a]), batch=batch, seq=seq)
        else:
            b = layer - n_a
            if k_shared is None:
                q, k_shared, v_shared = _proj(
                    h, [gain(layer, 2), kv_norm_g[None, :]], [bf(sb_w_q[b]), bf(w_k), bf(w_v)],
                    norm_of=(0, 1, 1), scales=(q_scale, 1.0, 1.0))
            else:
                (q,) = _proj(h, [gain(layer, 2)], [bf(sb_w_q[b])], norm_of=(0,), scales=(q_scale,))
            attn = (_attention(q, k_shared, v_shared, batch=batch, seq=seq), bf(sb_w_o[b]), gain(layer, 3))
        nxt = "lb" if (layer + 1 < n_a) else "bl"
        h = _ffn(h, gain(layer, 4), gain(layer, 5), bf(ffn_w_gate[layer, 1]), bf(ffn_w_up[layer, 1]),
                 bf(ffn_w_down[layer, 1]), batch=batch, seq=seq, src=order, dst=nxt, attn=attn)
        order = nxt
    return h.reshape(batch, seq, d)
```

```python
import functools
import math

import jax
import jax.numpy as jnp
from jax import lax
from jax.experimental import pallas as pl
from jax.experimental.pallas import tpu as pltpu

F32 = jnp.float32
BF16 = jnp.bfloat16

NORM_EPS = 1e-6
SSM_GROUP = 16
SSM_STATE = 64
SB_HEAD_DIM = 64
LANES = 128
CH_BLOCK = 256
GROUPS_PER_BLOCK = CH_BLOCK // SSM_GROUP
STATES_PER_BLOCK = GROUPS_PER_BLOCK * SSM_STATE
SCAN_COLS = 512
VMEM_LIMIT = 56 * 1024 * 1024

FFN_ROWS = 512
S5_STEPS = 32
ATT_Q = 256
ATT_K = 256
ATT_HEADS = 16
EXP_UNDERFLOW = 104.5
MASKED_SCORE = -1e30


def _rms(x, g):
    ms = jnp.mean(x * x, axis=-1, keepdims=True)
    return x * lax.rsqrt(ms + NORM_EPS) * g


def _dot(a, b):
    return jnp.dot(a, b, preferred_element_type=F32)


def _const_spec(shape):
    nd = len(shape)
    return pl.BlockSpec(shape, lambda *_: (0,) * nd, pipeline_mode=pl.Buffered(1))


def _ffn_kernel(*refs, attn_proj, n_norm, norm_of, scales):
    refs = list(refs)
    if attn_proj:
        a_ref, wo_ref, gm_ref = refs[:3]
        refs = refs[3:]
    x_ref, gi_ref, go_ref, wg_ref, wu_ref, wd_ref = refs[:6]
    n_proj = len(norm_of)
    pg_refs = refs[6:6 + n_norm]
    pw_refs = refs[6 + n_norm:6 + n_norm + n_proj]
    o_ref = refs[6 + n_norm + n_proj]
    po_refs = refs[7 + n_norm + n_proj:]
    half = x_ref.shape[0] // 2
    for rows in (slice(0, half), slice(half, 2 * half)):
        x = x_ref[rows, :]
        if attn_proj:
            x = x + _rms(_dot(a_ref[rows, :], wo_ref[...]), gm_ref[...])
        h = _rms(x, gi_ref[...]).astype(BF16)
        g = _dot(h, wg_ref[...])
        u = _dot(h, wu_ref[...])
        a = (g * jax.nn.sigmoid(g) * u).astype(BF16)
        y = _dot(a, wd_ref[...])
        out = x + 0.5 * _rms(y, go_ref[...])
        o_ref[rows, :] = out
        hs = [_rms(out, pg[...]).astype(BF16) for pg in pg_refs]
        for pw_ref, po_ref, gi, sc in zip(pw_refs, po_refs, norm_of, scales):
            p = _dot(hs[gi], pw_ref[...])
            if sc != 1.0:
                p = p * sc
            po_ref[rows, :] = p.astype(po_ref.dtype)


def _ffn(x, g_in, g_out, wg, wu, wd, *, batch, seq, src, dst, attn=None, proj=None):
    d = x.shape[-1]
    f = wg.shape[-1]
    tm = FFN_ROWS
    nl = seq // tm

    def spec(order):
        if order == "bl":
            return (batch * seq, d), pl.BlockSpec((tm, d), lambda b, l: (b * nl + l, 0))
        return (seq, batch * d), pl.BlockSpec((tm, d), lambda b, l: (l, b))

    if src == dst:
        n = (batch * seq) // tm
        grid = (n,)
        in_shape = out_shape = (batch * seq, d)
        x_spec = o_spec = pl.BlockSpec((tm, d), lambda i: (i, 0))
    else:
        grid = (batch, nl)
        in_shape, x_spec = spec(src)
        out_shape, o_spec = spec(dst)
    operands = [x.reshape(in_shape), g_in, g_out, wg, wu, wd]
    in_specs = [x_spec, _const_spec((1, d)), _const_spec((1, d)),
                _const_spec((d, f)), _const_spec((d, f)), _const_spec((f, d))]
    out_specs = [o_spec]
    out_shapes = [jax.ShapeDtypeStruct(out_shape, F32)]
    if attn is not None:
        assert src == dst
        a, w_o, g_mix = attn
        operands = [a, w_o, g_mix] + operands
        in_specs = [pl.BlockSpec((tm, a.shape[1]), lambda i: (i, 0)), _const_spec(w_o.shape),
                    _const_spec((1, d))] + in_specs
    gains, weights, norm_of, scales = proj if proj is not None else ((), (), (), ())
    if proj is not None:
        assert src == dst
        operands += list(gains) + list(weights)
        in_specs += [_const_spec((1, d))] * len(gains) + [_const_spec(w.shape) for w in weights]
        out_specs += [pl.BlockSpec((tm, w.shape[1]), lambda i: (i, 0)) for w in weights]
        out_shapes += [jax.ShapeDtypeStruct((batch * seq, w.shape[1]), BF16) for w in weights]
    outs = pl.pallas_call(
        functools.partial(_ffn_kernel, attn_proj=attn is not None, n_norm=len(gains),
                          norm_of=tuple(norm_of), scales=tuple(scales)),
        grid=grid,
        in_specs=in_specs,
        out_specs=out_specs,
        out_shape=out_shapes,
        compiler_params=pltpu.CompilerParams(
            dimension_semantics=("arbitrary",) * len(grid), vmem_limit_bytes=VMEM_LIMIT),
        name="ffn",
    )(*operands)
    return (outs[0].reshape(batch * seq, d), *outs[1:])


def _s5_param_kernel(lr_ref, li_ref, ldt_ref, br_ref, bi_ref, ar_ref, ai_ref, bbr_ref, bbi_ref):
    lr = lr_ref[...]
    li = li_ref[...]
    dt = jnp.exp(ldt_ref[...])
    mag = jnp.exp(lr * dt)
    ab_re = mag * jnp.cos(li * dt)
    ab_im = mag * jnp.sin(li * dt)
    den = lr * lr + li * li
    nr = ab_re - 1.0
    ni = ab_im
    f_re = (nr * lr + ni * li) / den
    f_im = (ni * lr - nr * li) / den
    b_re = br_ref[...]
    b_im = bi_ref[...]
    ar_ref[...] = ab_re
    ai_ref[...] = ab_im
    bbr_ref[...] = f_re * b_re - f_im * b_im
    bbi_ref[...] = f_re * b_im + f_im * b_re


def _s5_params(lam_re, lam_im, log_dt, b_re, b_im, c_re, c_im):
    g, n = lam_re.shape
    c = b_re.shape[-1]
    rep = lambda t: jnp.repeat(t, c, axis=0)
    b_t = lambda t: jnp.transpose(t, (0, 2, 1)).reshape(g * c, n)
    shp = jax.ShapeDtypeStruct((g * c, n), F32)
    ab_re, ab_im, bb_re, bb_im = pl.pallas_call(
        _s5_param_kernel, out_shape=(shp,) * 4, name="s5_params",
    )(rep(lam_re), rep(lam_im), rep(log_dt[:, None]), b_t(b_re), b_t(b_im))
    a_re = ab_re[::c].reshape(1, g * n)
    a_im = ab_im[::c].reshape(1, g * n)
    nb = g // GROUPS_PER_BLOCK
    eye = jnp.eye(GROUPS_PER_BLOCK, dtype=F32)

    def b_block(t):
        t = t.reshape(nb, GROUPS_PER_BLOCK, c, n)
        return jnp.einsum("jgcn,gh->jgchn", t, eye).reshape(nb, CH_BLOCK, STATES_PER_BLOCK)

    def c_block(t):
        t = t.reshape(nb, GROUPS_PER_BLOCK, c, n)
        return jnp.einsum("jgcn,gh->jhngc", t, eye).reshape(nb, STATES_PER_BLOCK, CH_BLOCK)

    b_mat = jnp.concatenate([b_block(bb_re), b_block(bb_im)], axis=2).astype(BF16)
    c_mat = jnp.concatenate([c_block(c_re), -c_block(c_im)], axis=1).astype(BF16)
    return a_re, a_im, b_mat, c_mat


def _gelu_tanh(y):
    return 0.5 * y * (1.0 + jnp.tanh(math.sqrt(2.0 / math.pi) * (y + 0.044715 * (y * y * y))))


def _s5_kernel(x_ref, gi_ref, go_ref, win_ref, bmat_ref, are_ref, aim_ref, cmat_ref, d_ref,
               wglu_ref, wout_ref, o_ref, sre_ref, sim_ref, bu_ref, sbf_ref, *, batch, steps):
    @pl.when(pl.program_id(0) == 0)
    def _():
        sre_ref[...] = jnp.zeros_like(sre_ref)
        sim_ref[...] = jnp.zeros_like(sim_ref)

    x = x_ref[...]
    hn = _rms(x, gi_ref[...]).astype(BF16)
    u = _dot(hn, win_ref[...])
    u_bf = u.astype(BF16)
    nb = bmat_ref.shape[0]
    spb = STATES_PER_BLOCK
    y_parts = []
    for j in range(nb):
        bu_ref[...] = _dot(u_bf[:, j * CH_BLOCK:(j + 1) * CH_BLOCK], bmat_ref[j])
        for c0 in range(0, spb, SCAN_COLS):
            cols = slice(j * spb + c0, j * spb + c0 + SCAN_COLS)
            a_re = jnp.broadcast_to(are_ref[:, cols], (batch, SCAN_COLS))
            a_im = jnp.broadcast_to(aim_ref[:, cols], (batch, SCAN_COLS))

            def step(t, carry, c0=c0, a_re=a_re, a_im=a_im):
                s_re, s_im = carry
                rows = pl.ds(pl.multiple_of(t * batch, batch), batch)
                n_re = a_re * s_re - a_im * s_im + bu_ref[rows, c0:c0 + SCAN_COLS]
                n_im = a_re * s_im + a_im * s_re + bu_ref[rows, spb + c0:spb + c0 + SCAN_COLS]
                sbf_ref[rows, c0:c0 + SCAN_COLS] = n_re.astype(BF16)
                sbf_ref[rows, spb + c0:spb + c0 + SCAN_COLS] = n_im.astype(BF16)
                return n_re, n_im

            s_re, s_im = lax.fori_loop(0, steps, step, (sre_ref[:, cols], sim_ref[:, cols]),
                                       unroll=True)
            sre_ref[:, cols] = s_re
            sim_ref[:, cols] = s_im
        y_parts.append(_dot(sbf_ref[...], cmat_ref[j]))
    y = jnp.concatenate(y_parts, axis=1) + d_ref[...] * u
    y = _gelu_tanh(y)
    z = y * jax.nn.sigmoid(_dot(y.astype(BF16), wglu_ref[...]))
    mix = _dot(z.astype(BF16), wout_ref[...])
    o_ref[...] = x + _rms(mix, go_ref[...])


def _s5(x, g_in, g_out, w_in, params, d_skip, w_glu, w_out, *, batch, seq):
    a_re, a_im, b_mat, c_mat = params
    d = x.shape[-1]
    rows = S5_STEPS * batch
    n_states = a_re.shape[-1]
    kern = functools.partial(_s5_kernel, batch=batch, steps=S5_STEPS)
    row_spec = pl.BlockSpec((rows, d), lambda i: (i, 0))
    return pl.pallas_call(
        kern,
        grid=(seq // S5_STEPS,),
        in_specs=[row_spec, _const_spec((1, d)), _const_spec((1, d)), _const_spec(w_in.shape),
                  _const_spec(b_mat.shape), _const_spec(a_re.shape), _const_spec(a_im.shape),
                  _const_spec(c_mat.shape), _const_spec((1, d)),
                  _const_spec(w_glu.shape), _const_spec(w_out.shape)],
        out_specs=row_spec,
        out_shape=jax.ShapeDtypeStruct(x.shape, F32),
        scratch_shapes=[pltpu.VMEM((batch, n_states), F32), pltpu.VMEM((batch, n_states), F32),
                        pltpu.VMEM((rows, 2 * STATES_PER_BLOCK), F32),
                        pltpu.VMEM((rows, 2 * STATES_PER_BLOCK), BF16)],
        compiler_params=pltpu.CompilerParams(
            dimension_semantics=("arbitrary",), vmem_limit_bytes=VMEM_LIMIT),
        name="s5_mixer",
    )(x, g_in, g_out, w_in, b_mat, a_re, a_im, c_mat, d_skip, w_glu, w_out)


def _softplus(z):
    sign_bit = jnp.uint32(0x80000000)
    neg_abs = lax.bitcast_convert_type(lax.bitcast_convert_type(z, jnp.uint32) | sign_bit, F32)
    return jnp.maximum(z, 0.0) + jnp.log(1.0 + jnp.exp(neg_abs))


def _attn_kernel(q_ref, k_ref, v_ref, o_ref, acc_ref, c_ref):
    tk = ATT_K
    n_heads = q_ref.shape[1] // SB_HEAD_DIM
    qi = pl.program_id(2)
    lane = lax.broadcasted_iota(jnp.int32, (1, LANES), 1)
    first_head = lane < SB_HEAD_DIM
    q_heads = []
    for p in range(n_heads // 2):
        q = q_ref[:, p * LANES:(p + 1) * LANES]
        zero = jnp.zeros_like(q)
        q_heads += [jnp.where(first_head, q, zero), jnp.where(first_head, zero, q)]
    r = lax.broadcasted_iota(jnp.int32, (tk, tk), 0)
    c = lax.broadcasted_iota(jnp.int32, (tk, tk), 1)
    later = jnp.where(r > c, 1.0, 0.0).astype(BF16)
    causal = c < r

    def block(kb, c_prev, diag):
        ks = pl.ds(pl.multiple_of(kb * tk, tk), tk)
        outs = []
        for h in range(n_heads):
            pair = slice(h // 2 * LANES, (h // 2 + 1) * LANES)
            k = k_ref[ks, pair]
            v = v_ref[ks, pair]
            z = lax.dot_general(q_heads[h], k, (((1,), (1,)), ((), ())), preferred_element_type=F32)
            if diag:
                z = jnp.where(causal, z, MASKED_SCORE)
            drop = _softplus(z)
            incl = _dot(drop.astype(BF16), later) + drop
            log_w = z - incl
            if c_prev is not None:
                log_w = log_w - c_prev[h]
            w = jnp.exp(log_w)
            outs.append((_dot(w.astype(BF16), v), incl[:, 0:1]))
        return outs

    @pl.when(qi == 0)
    def _():
        d = block(0, None, True)
        for h in range(n_heads):
            acc_ref[h] = d[h][0]

    @pl.when(qi > 0)
    def _():
        d = block(qi, None, True)
        p = block(qi - 1, [d[h][1] for h in range(n_heads)], False)
        for h in range(n_heads):
            acc_ref[h] = d[h][0] + p[h][0]
            c_ref[h] = d[h][1] + p[h][1]

        def cond(carry):
            kb, c_min = carry
            return jnp.logical_and(kb >= 0, c_min <= EXP_UNDERFLOW)

        def body(carry):
            kb, _ = carry
            o = block(kb, [c_ref[h] for h in range(n_heads)], False)
            c_new = [c_ref[h] + o[h][1] for h in range(n_heads)]
            for h in range(n_heads):
                acc_ref[h] += o[h][0]
                c_ref[h] = c_new[h]
            return kb - 1, jnp.min(functools.reduce(jnp.minimum, c_new))

        lax.while_loop(cond, body, (qi - 2, jnp.min(c_ref[...])))

    for p in range(n_heads // 2):
        o_ref[:, p * LANES:(p + 1) * LANES] = jnp.where(
            first_head, acc_ref[2 * p], acc_ref[2 * p + 1]).astype(o_ref.dtype)


def _attention(q, k, v, *, batch, seq):
    t, width = q.shape
    assert ATT_Q == ATT_K
    nq = seq // ATT_Q
    lanes = ATT_HEADS * SB_HEAD_DIM
    kv_spec = pl.BlockSpec((seq, lanes), lambda b, p, i: (b, p))
    q_spec = pl.BlockSpec((ATT_Q, lanes), lambda b, p, i: (b * nq + i, p))
    return pl.pallas_call(
        _attn_kernel,
        grid=(batch, width // lanes, nq),
        in_specs=[q_spec, kv_spec, kv_spec],
        out_specs=q_spec,
        out_shape=jax.ShapeDtypeStruct((t, width), BF16),
        scratch_shapes=[pltpu.VMEM((ATT_HEADS, ATT_Q, LANES), F32), pltpu.VMEM((ATT_HEADS, ATT_Q, 1), F32)],
        compiler_params=pltpu.CompilerParams(
            dimension_semantics=("arbitrary",) * 3, vmem_limit_bytes=VMEM_LIMIT),
        name="sb_attention",
    )(q, k, v)


def kernel(x, norm_g, ffn_w_gate, ffn_w_up, ffn_w_down, ssm_w_in, ssm_lam_re, ssm_lam_im, ssm_log_dt, ssm_b_re, ssm_b_im, ssm_c_re, ssm_c_im, ssm_d, ssm_w_glu, ssm_w_out, kv_norm_g, w_k, w_v, sb_w_q, sb_w_o):
    batch, seq, d = x.shape
    depth = norm_g.shape[0]
    n_a = ssm_w_in.shape[0]
    bf = lambda t: t.astype(BF16)
    gain = lambda layer, i: norm_g[layer, i][None, :]
    q_scale = 1.0 / math.sqrt(SB_HEAD_DIM)

    h = x.reshape(batch * seq, d)
    order = "bl"
    k_shared = v_shared = None
    for layer in range(depth):
        is_ssm = layer < n_a
        attn = proj = None
        want = "lb" if is_ssm else "bl"
        if not is_ssm:
            b = layer - n_a
            if k_shared is None:
                proj = ([gain(layer, 2), kv_norm_g[None, :]], [bf(sb_w_q[b]), bf(w_k), bf(w_v)],
                        (0, 1, 1), (q_scale, 1.0, 1.0))
            else:
                proj = ([gain(layer, 2)], [bf(sb_w_q[b])], (0,), (q_scale,))
        h, *projected = _ffn(h, gain(layer, 0), gain(layer, 1), bf(ffn_w_gate[layer, 0]),
                             bf(ffn_w_up[layer, 0]), bf(ffn_w_down[layer, 0]), batch=batch, seq=seq,
                             src=order, dst=want, proj=proj)
        order = want
        if is_ssm:
            a = layer
            params = _s5_params(ssm_lam_re[a], ssm_lam_im[a], ssm_log_dt[a], ssm_b_re[a], ssm_b_im[a],
                                ssm_c_re[a], ssm_c_im[a])
            h = _s5(h, gain(layer, 2), gain(layer, 3), bf(ssm_w_in[a]), params,
                    ssm_d[a].reshape(1, d), bf(ssm_w_glu[a]), bf(ssm_w_out[a]), batch=batch, seq=seq)
        else:
            if len(projected) == 3:
                q, k_shared, v_shared = projected
            else:
                (q,) = projected
            attn = (_attention(q, k_shared, v_shared, batch=batch, seq=seq), bf(sb_w_o[b]), gain(layer, 3))
        nxt = "lb" if (layer + 1 < n_a) else "bl"
        (h,) = _ffn(h, gain(layer, 4), gain(layer, 5), bf(ffn_w_gate[layer, 1]), bf(ffn_w_up[layer, 1]),
                    bf(ffn_w_down[layer, 1]), batch=batch, seq=seq, src=order, dst=nxt, attn=attn)
        order = nxt
    return h.reshape(batch, seq, d)
```

```python
import functools
import math

import jax
import jax.numpy as jnp
from jax import lax
from jax.experimental import pallas as pl
from jax.experimental.pallas import tpu as pltpu

F32 = jnp.float32
BF16 = jnp.bfloat16

NORM_EPS = 1e-6
SSM_GROUP = 16
SSM_STATE = 64
SB_HEAD_DIM = 64
LANES = 128
SUBLANES = 8
CH_BLOCK = 256
GROUPS_PER_BLOCK = CH_BLOCK // SSM_GROUP
STATES_PER_BLOCK = GROUPS_PER_BLOCK * SSM_STATE
SCAN_COLS = 512
VMEM_LIMIT = 56 * 1024 * 1024

FFN_ROWS = 512
S5_STEPS = 32
ATT_Q = 256
ATT_K = 256
ATT_HEADS = 8
EXP_UNDERFLOW = 104.5
MASKED_SCORE = -1e30


def _rms(x, g):
    ms = jnp.mean(x * x, axis=-1, keepdims=True)
    return x * lax.rsqrt(ms + NORM_EPS) * g


def _dot(a, b):
    return jnp.dot(a, b, preferred_element_type=F32)


def _const_spec(shape):
    nd = len(shape)
    return pl.BlockSpec(shape, lambda *_: (0,) * nd, pipeline_mode=pl.Buffered(1))


def _ffn_kernel(*refs, attn_proj, n_norm, norm_of, scales, in_order, out_order):
    refs = list(refs)
    if attn_proj:
        a_ref, wo_ref, gm_ref = refs[:3]
        refs = refs[3:]
    x_ref, gi_ref, go_ref, wg_ref, wu_ref, wd_ref = refs[:6]
    n_proj = len(norm_of)
    pg_refs = refs[6:6 + n_norm]
    pw_refs = refs[6 + n_norm:6 + n_norm + n_proj]
    o_ref = refs[6 + n_norm + n_proj]
    po_refs = refs[7 + n_norm + n_proj:]
    d = x_ref.shape[-1]
    n_rows = x_ref.size // d
    half = n_rows // 2
    t_half = half // SUBLANES
    for i in range(2):
        rows = slice(i * half, (i + 1) * half)
        steps = slice(i * t_half, (i + 1) * t_half)
        if in_order is None:
            x = x_ref[rows, :]
        elif in_order == "bt":
            x = x_ref[:, steps, :].reshape(half, d)
        else:
            x = pltpu.einshape("tbd->btd", x_ref[steps]).reshape(half, d)
        if attn_proj:
            x = x + _rms(_dot(a_ref[rows, :], wo_ref[...]), gm_ref[...])
        h = _rms(x, gi_ref[...]).astype(BF16)
        g = _dot(h, wg_ref[...])
        u = _dot(h, wu_ref[...])
        a = (g * jax.nn.sigmoid(g) * u).astype(BF16)
        y = _dot(a, wd_ref[...])
        out = x + 0.5 * _rms(y, go_ref[...])
        if out_order is None:
            o_ref[rows, :] = out
        elif out_order == "bt":
            o_ref[:, steps, :] = out.reshape(SUBLANES, t_half, d)
        else:
            o_ref[steps] = pltpu.einshape("btd->tbd", out.reshape(SUBLANES, t_half, d))
        hs = [_rms(out, pg[...]).astype(BF16) for pg in pg_refs]
        for pw_ref, po_ref, gi, sc in zip(pw_refs, po_refs, norm_of, scales):
            p = _dot(hs[gi], pw_ref[...])
            if sc != 1.0:
                p = p * sc
            po_ref[rows, :] = p.astype(po_ref.dtype)


def _ffn(x, g_in, g_out, wg, wu, wd, *, batch, seq, src, dst, attn=None, proj=None):
    d = x.shape[-1]
    f = wg.shape[-1]
    tm = FFN_ROWS
    tl = tm // SUBLANES

    def spec(order):
        if order == "bl":
            return (batch, seq, d), pl.BlockSpec((SUBLANES, tl, d), lambda g, l: (g, l, 0)), "bt"
        return (seq, batch, d), pl.BlockSpec((tl, SUBLANES, d), lambda g, l: (l, g, 0)), "tb"

    if src == dst:
        n = (batch * seq) // tm
        grid = (n,)
        in_shape = out_shape = (batch * seq, d)
        x_spec = o_spec = pl.BlockSpec((tm, d), lambda i: (i, 0))
        in_order = out_order = None
    else:
        grid = (batch // SUBLANES, seq // tl)
        in_shape, x_spec, in_order = spec(src)
        out_shape, o_spec, out_order = spec(dst)
    operands = [x.reshape(in_shape), g_in, g_out, wg, wu, wd]
    in_specs = [x_spec, _const_spec((1, d)), _const_spec((1, d)),
                _const_spec((d, f)), _const_spec((d, f)), _const_spec((f, d))]
    out_specs = [o_spec]
    out_shapes = [jax.ShapeDtypeStruct(out_shape, F32)]
    if attn is not None:
        assert src == dst
        a, w_o, g_mix = attn
        operands = [a, w_o, g_mix] + operands
        in_specs = [pl.BlockSpec((tm, a.shape[1]), lambda i: (i, 0)), _const_spec(w_o.shape),
                    _const_spec((1, d))] + in_specs
    gains, weights, norm_of, scales = proj if proj is not None else ((), (), (), ())
    if proj is not None:
        assert src == dst
        operands += list(gains) + list(weights)
        in_specs += [_const_spec((1, d))] * len(gains) + [_const_spec(w.shape) for w in weights]
        out_specs += [pl.BlockSpec((tm, w.shape[1]), lambda i: (i, 0)) for w in weights]
        out_shapes += [jax.ShapeDtypeStruct((batch * seq, w.shape[1]), BF16) for w in weights]
    outs = pl.pallas_call(
        functools.partial(_ffn_kernel, attn_proj=attn is not None, n_norm=len(gains),
                          norm_of=tuple(norm_of), scales=tuple(scales), in_order=in_order,
                          out_order=out_order),
        grid=grid,
        in_specs=in_specs,
        out_specs=out_specs,
        out_shape=out_shapes,
        compiler_params=pltpu.CompilerParams(
            dimension_semantics=("arbitrary",) * len(grid), vmem_limit_bytes=VMEM_LIMIT),
        name="ffn",
    )(*operands)
    return (outs[0].reshape(batch * seq, d), *outs[1:])


def _s5_param_kernel(lr_ref, li_ref, ldt_ref, br_ref, bi_ref, ar_ref, ai_ref, bbr_ref, bbi_ref):
    lr = lr_ref[...]
    li = li_ref[...]
    dt = jnp.exp(ldt_ref[...])
    mag = jnp.exp(lr * dt)
    ab_re = mag * jnp.cos(li * dt)
    ab_im = mag * jnp.sin(li * dt)
    den = lr * lr + li * li
    nr = ab_re - 1.0
    ni = ab_im
    f_re = (nr * lr + ni * li) / den
    f_im = (ni * lr - nr * li) / den
    b_re = br_ref[...]
    b_im = bi_ref[...]
    ar_ref[...] = ab_re
    ai_ref[...] = ab_im
    bbr_ref[...] = f_re * b_re - f_im * b_im
    bbi_ref[...] = f_re * b_im + f_im * b_re


def _s5_params(lam_re, lam_im, log_dt, b_re, b_im, c_re, c_im):
    g, n = lam_re.shape
    c = b_re.shape[-1]
    rep = lambda t: jnp.repeat(t, c, axis=0)
    b_t = lambda t: jnp.transpose(t, (0, 2, 1)).reshape(g * c, n)
    shp = jax.ShapeDtypeStruct((g * c, n), F32)
    ab_re, ab_im, bb_re, bb_im = pl.pallas_call(
        _s5_param_kernel, out_shape=(shp,) * 4, name="s5_params",
    )(rep(lam_re), rep(lam_im), rep(log_dt[:, None]), b_t(b_re), b_t(b_im))
    a_re = ab_re[::c].reshape(1, g * n)
    a_im = ab_im[::c].reshape(1, g * n)
    nb = g // GROUPS_PER_BLOCK
    eye = jnp.eye(GROUPS_PER_BLOCK, dtype=F32)

    def b_block(t):
        t = t.reshape(nb, GROUPS_PER_BLOCK, c, n)
        return jnp.einsum("jgcn,gh->jgchn", t, eye).reshape(nb, CH_BLOCK, STATES_PER_BLOCK)

    def c_block(t):
        t = t.reshape(nb, GROUPS_PER_BLOCK, c, n)
        return jnp.einsum("jgcn,gh->jhngc", t, eye).reshape(nb, STATES_PER_BLOCK, CH_BLOCK)

    b_mat = jnp.concatenate([b_block(bb_re), b_block(bb_im)], axis=2).astype(BF16)
    c_mat = jnp.concatenate([c_block(c_re), -c_block(c_im)], axis=1).astype(BF16)
    return a_re, a_im, b_mat, c_mat


def _gelu_tanh(y):
    return 0.5 * y * (1.0 + jnp.tanh(math.sqrt(2.0 / math.pi) * (y + 0.044715 * (y * y * y))))


def _s5_kernel(x_ref, gi_ref, go_ref, win_ref, bmat_ref, are_ref, aim_ref, cmat_ref, d_ref,
               wglu_ref, wout_ref, o_ref, sre_ref, sim_ref, bu_ref, sbf_ref, *, batch, steps):
    @pl.when(pl.program_id(0) == 0)
    def _():
        sre_ref[...] = jnp.zeros_like(sre_ref)
        sim_ref[...] = jnp.zeros_like(sim_ref)

    x = x_ref[...]
    hn = _rms(x, gi_ref[...]).astype(BF16)
    u = _dot(hn, win_ref[...])
    u_bf = u.astype(BF16)
    nb = bmat_ref.shape[0]
    spb = STATES_PER_BLOCK
    y_parts = []
    for j in range(nb):
        bu_ref[...] = _dot(u_bf[:, j * CH_BLOCK:(j + 1) * CH_BLOCK], bmat_ref[j])
        for c0 in range(0, spb, SCAN_COLS):
            cols = slice(j * spb + c0, j * spb + c0 + SCAN_COLS)
            a_re = jnp.broadcast_to(are_ref[:, cols], (batch, SCAN_COLS))
            a_im = jnp.broadcast_to(aim_ref[:, cols], (batch, SCAN_COLS))

            def step(t, carry, c0=c0, a_re=a_re, a_im=a_im):
                s_re, s_im = carry
                rows = pl.ds(pl.multiple_of(t * batch, batch), batch)
                n_re = a_re * s_re - a_im * s_im + bu_ref[rows, c0:c0 + SCAN_COLS]
                n_im = a_re * s_im + a_im * s_re + bu_ref[rows, spb + c0:spb + c0 + SCAN_COLS]
                sbf_ref[rows, c0:c0 + SCAN_COLS] = n_re.astype(BF16)
                sbf_ref[rows, spb + c0:spb + c0 + SCAN_COLS] = n_im.astype(BF16)
                return n_re, n_im

            s_re, s_im = lax.fori_loop(0, steps, step, (sre_ref[:, cols], sim_ref[:, cols]),
                                       unroll=True)
            sre_ref[:, cols] = s_re
            sim_ref[:, cols] = s_im
        y_parts.append(_dot(sbf_ref[...], cmat_ref[j]))
    y = jnp.concatenate(y_parts, axis=1) + d_ref[...] * u
    y = _gelu_tanh(y)
    z = y * jax.nn.sigmoid(_dot(y.astype(BF16), wglu_ref[...]))
    mix = _dot(z.astype(BF16), wout_ref[...])
    o_ref[...] = x + _rms(mix, go_ref[...])


def _s5(x, g_in, g_out, w_in, params, d_skip, w_glu, w_out, *, batch, seq):
    a_re, a_im, b_mat, c_mat = params
    d = x.shape[-1]
    rows = S5_STEPS * batch
    n_states = a_re.shape[-1]
    kern = functools.partial(_s5_kernel, batch=batch, steps=S5_STEPS)
    row_spec = pl.BlockSpec((rows, d), lambda i: (i, 0))
    return pl.pallas_call(
        kern,
        grid=(seq // S5_STEPS,),
        in_specs=[row_spec, _const_spec((1, d)), _const_spec((1, d)), _const_spec(w_in.shape),
                  _const_spec(b_mat.shape), _const_spec(a_re.shape), _const_spec(a_im.shape),
                  _const_spec(c_mat.shape), _const_spec((1, d)),
                  _const_spec(w_glu.shape), _const_spec(w_out.shape)],
        out_specs=row_spec,
        out_shape=jax.ShapeDtypeStruct(x.shape, F32),
        scratch_shapes=[pltpu.VMEM((batch, n_states), F32), pltpu.VMEM((batch, n_states), F32),
                        pltpu.VMEM((rows, 2 * STATES_PER_BLOCK), F32),
                        pltpu.VMEM((rows, 2 * STATES_PER_BLOCK), BF16)],
        compiler_params=pltpu.CompilerParams(
            dimension_semantics=("arbitrary",), vmem_limit_bytes=VMEM_LIMIT),
        name="s5_mixer",
    )(x, g_in, g_out, w_in, b_mat, a_re, a_im, c_mat, d_skip, w_glu, w_out)


def _softplus(z):
    sign_bit = jnp.uint32(0x80000000)
    neg_abs = lax.bitcast_convert_type(lax.bitcast_convert_type(z, jnp.uint32) | sign_bit, F32)
    return jnp.maximum(z, 0.0) + jnp.log(1.0 + jnp.exp(neg_abs))


def _attn_kernel(q_ref, k_ref, v_ref, o_ref, acc_ref, c_ref):
    tk = ATT_K
    n_heads = q_ref.shape[1] // SB_HEAD_DIM
    qi = pl.program_id(2)
    lane = lax.broadcasted_iota(jnp.int32, (1, LANES), 1)
    first_head = lane < SB_HEAD_DIM
    q_heads = []
    for p in range(n_heads // 2):
        q = q_ref[:, p * LANES:(p + 1) * LANES]
        zero = jnp.zeros_like(q)
        q_heads += [jnp.where(first_head, q, zero), jnp.where(first_head, zero, q)]
    r = lax.broadcasted_iota(jnp.int32, (tk, tk), 0)
    c = lax.broadcasted_iota(jnp.int32, (tk, tk), 1)
    later = jnp.where(r > c, 1.0, 0.0).astype(BF16)
    causal = c < r

    def block(kb, c_prev, diag):
        ks = pl.ds(pl.multiple_of(kb * tk, tk), tk)
        outs = []
        for h in range(n_heads):
            pair = slice(h // 2 * LANES, (h // 2 + 1) * LANES)
            k = k_ref[ks, pair]
            v = v_ref[ks, pair]
            z = lax.dot_general(q_heads[h], k, (((1,), (1,)), ((), ())), preferred_element_type=F32)
            if diag:
                z = jnp.where(causal, z, MASKED_SCORE)
            drop = _softplus(z)
            incl = _dot(drop.astype(BF16), later) + drop
            log_w = z - incl
            if c_prev is not None:
                log_w = log_w - c_prev[h]
            w = jnp.exp(log_w)
            outs.append((_dot(w.astype(BF16), v), incl[:, 0:1]))
        return outs

    @pl.when(qi == 0)
    def _():
        d = block(0, None, True)
        for h in range(n_heads):
            acc_ref[h] = d[h][0]

    @pl.when(qi > 0)
    def _():
        d = block(qi, None, True)
        p = block(qi - 1, [d[h][1] for h in range(n_heads)], False)
        for h in range(n_heads):
            acc_ref[h] = d[h][0] + p[h][0]
            c_ref[h] = d[h][1] + p[h][1]

        def cond(carry):
            kb, c_min = carry
            return jnp.logical_and(kb >= 0, c_min <= EXP_UNDERFLOW)

        def body(carry):
            kb, _ = carry
            o = block(kb, [c_ref[h] for h in range(n_heads)], False)
            c_new = [c_ref[h] + o[h][1] for h in range(n_heads)]
            for h in range(n_heads):
                acc_ref[h] += o[h][0]
                c_ref[h] = c_new[h]
            return kb - 1, jnp.min(functools.reduce(jnp.minimum, c_new))

        lax.while_loop(cond, body, (qi - 2, jnp.min(c_ref[...])))

    for p in range(n_heads // 2):
        o_ref[:, p * LANES:(p + 1) * LANES] = jnp.where(
            first_head, acc_ref[2 * p], acc_ref[2 * p + 1]).astype(o_ref.dtype)


def _attention(q, k, v, *, batch, seq):
    t, width = q.shape
    assert ATT_Q == ATT_K
    nq = seq // ATT_Q
    lanes = ATT_HEADS * SB_HEAD_DIM
    kv_spec = pl.BlockSpec((seq, lanes), lambda b, p, i: (b, p))
    q_spec = pl.BlockSpec((ATT_Q, lanes), lambda b, p, i: (b * nq + i, p))
    return pl.pallas_call(
        _attn_kernel,
        grid=(batch, width // lanes, nq),
        in_specs=[q_spec, kv_spec, kv_spec],
        out_specs=q_spec,
        out_shape=jax.ShapeDtypeStruct((t, width), BF16),
        scratch_shapes=[pltpu.VMEM((ATT_HEADS, ATT_Q, LANES), F32), pltpu.VMEM((ATT_HEADS, ATT_Q, 1), F32)],
        compiler_params=pltpu.CompilerParams(
            dimension_semantics=("arbitrary",) * 3, vmem_limit_bytes=VMEM_LIMIT),
        name="sb_attention",
    )(q, k, v)


def kernel(x, norm_g, ffn_w_gate, ffn_w_up, ffn_w_down, ssm_w_in, ssm_lam_re, ssm_lam_im, ssm_log_dt, ssm_b_re, ssm_b_im, ssm_c_re, ssm_c_im, ssm_d, ssm_w_glu, ssm_w_out, kv_norm_g, w_k, w_v, sb_w_q, sb_w_o):
    batch, seq, d = x.shape
    depth = norm_g.shape[0]
    n_a = ssm_w_in.shape[0]
    bf = lambda t: t.astype(BF16)
    gain = lambda layer, i: norm_g[layer, i][None, :]
    q_scale = 1.0 / math.sqrt(SB_HEAD_DIM)

    h = x.reshape(batch * seq, d)
    order = "bl"
    k_shared = v_shared = None
    for layer in range(depth):
        is_ssm = layer < n_a
        attn = proj = None
        want = "lb" if is_ssm else "bl"
        if not is_ssm:
            b = layer - n_a
            if k_shared is None:
                proj = ([gain(layer, 2), kv_norm_g[None, :]], [bf(sb_w_q[b]), bf(w_k), bf(w_v)],
                        (0, 1, 1), (q_scale, 1.0, 1.0))
            else:
                proj = ([gain(layer, 2)], [bf(sb_w_q[b])], (0,), (q_scale,))
        h, *projected = _ffn(h, gain(layer, 0), gain(layer, 1), bf(ffn_w_gate[layer, 0]),
                             bf(ffn_w_up[layer, 0]), bf(ffn_w_down[layer, 0]), batch=batch, seq=seq,
                             src=order, dst=want, proj=proj)
        order = want
        if is_ssm:
            a = layer
            params = _s5_params(ssm_lam_re[a], ssm_lam_im[a], ssm_log_dt[a], ssm_b_re[a], ssm_b_im[a],
                                ssm_c_re[a], ssm_c_im[a])
            h = _s5(h, gain(layer, 2), gain(layer, 3), bf(ssm_w_in[a]), params,
                    ssm_d[a].reshape(1, d), bf(ssm_w_glu[a]), bf(ssm_w_out[a]), batch=batch, seq=seq)
        else:
            if len(projected) == 3:
                q, k_shared, v_shared = projected
            else:
                (q,) = projected
            attn = (_attention(q, k_shared, v_shared, batch=batch, seq=seq), bf(sb_w_o[b]), gain(layer, 3))
        nxt = "lb" if (layer + 1 < n_a) else "bl"
        (h,) = _ffn(h, gain(layer, 4), gain(layer, 5), bf(ffn_w_gate[layer, 1]), bf(ffn_w_up[layer, 1]),
                    bf(ffn_w_down[layer, 1]), batch=batch, seq=seq, src=order, dst=nxt, attn=attn)
        order = nxt
    return h.reshape(batch, seq, d)
```

```python
import functools
import math

import jax
import jax.numpy as jnp
from jax import lax
from jax.experimental import pallas as pl
from jax.experimental.pallas import tpu as pltpu

F32 = jnp.float32
BF16 = jnp.bfloat16

NORM_EPS = 1e-6
SSM_GROUP = 16
SSM_STATE = 64
SB_HEAD_DIM = 64
LANES = 128
SUBLANES = 8
CH_BLOCK = 256
GROUPS_PER_BLOCK = CH_BLOCK // SSM_GROUP
STATES_PER_BLOCK = GROUPS_PER_BLOCK * SSM_STATE
SCAN_COLS = 512
VMEM_LIMIT = 56 * 1024 * 1024

FFN_ROWS = 1024
FFN_PARTS = 4
S5_STEPS = 32
ATT_Q = 256
ATT_K = 256
ATT_HEADS = 8
EXP_UNDERFLOW = 104.5
MASKED_SCORE = -1e30


def _rms(x, g):
    ms = jnp.mean(x * x, axis=-1, keepdims=True)
    return x * lax.rsqrt(ms + NORM_EPS) * g


def _dot(a, b):
    return jnp.dot(a, b, preferred_element_type=F32)


def _const_spec(shape):
    nd = len(shape)
    return pl.BlockSpec(shape, lambda *_: (0,) * nd, pipeline_mode=pl.Buffered(1))


def _ffn_kernel(*refs, attn_proj, n_norm, norm_of, scales, in_order, out_order):
    refs = list(refs)
    if attn_proj:
        a_ref, wo_ref, gm_ref = refs[:3]
        refs = refs[3:]
    x_ref, gi_ref, go_ref, wg_ref, wu_ref, wd_ref = refs[:6]
    n_proj = len(norm_of)
    pg_refs = refs[6:6 + n_norm]
    pw_refs = refs[6 + n_norm:6 + n_norm + n_proj]
    o_ref = refs[6 + n_norm + n_proj]
    po_refs = refs[7 + n_norm + n_proj:]
    d = x_ref.shape[-1]
    n_rows = x_ref.size // d
    part = n_rows // FFN_PARTS
    t_part = part // SUBLANES
    for i in range(FFN_PARTS):
        rows = slice(i * part, (i + 1) * part)
        steps = slice(i * t_part, (i + 1) * t_part)
        if in_order is None:
            x = x_ref[rows, :]
        elif in_order == "bt":
            x = x_ref[:, steps, :].reshape(part, d)
        else:
            x = pltpu.einshape("tbd->btd", x_ref[steps]).reshape(part, d)
        if attn_proj:
            x = x + _rms(_dot(a_ref[rows, :], wo_ref[...]), gm_ref[...])
        h = _rms(x, gi_ref[...]).astype(BF16)
        g = _dot(h, wg_ref[...])
        u = _dot(h, wu_ref[...])
        a = (g * jax.nn.sigmoid(g) * u).astype(BF16)
        y = _dot(a, wd_ref[...])
        out = x + 0.5 * _rms(y, go_ref[...])
        if out_order is None:
            o_ref[rows, :] = out
        elif out_order == "bt":
            o_ref[:, steps, :] = out.reshape(SUBLANES, t_part, d)
        else:
            o_ref[steps] = pltpu.einshape("btd->tbd", out.reshape(SUBLANES, t_part, d))
        hs = [_rms(out, pg[...]).astype(BF16) for pg in pg_refs]
        for pw_ref, po_ref, gi, sc in zip(pw_refs, po_refs, norm_of, scales):
            p = _dot(hs[gi], pw_ref[...])
            if sc != 1.0:
                p = p * sc
            po_ref[rows, :] = p.astype(po_ref.dtype)


def _ffn(x, g_in, g_out, wg, wu, wd, *, batch, seq, src, dst, attn=None, proj=None):
    d = x.shape[-1]
    f = wg.shape[-1]
    tm = FFN_ROWS
    tl = tm // SUBLANES

    def spec(order):
        if order == "bl":
            return (batch, seq, d), pl.BlockSpec((SUBLANES, tl, d), lambda g, l: (g, l, 0)), "bt"
        return (seq, batch, d), pl.BlockSpec((tl, SUBLANES, d), lambda g, l: (l, g, 0)), "tb"

    if src == dst:
        n = (batch * seq) // tm
        grid = (n,)
        in_shape = out_shape = (batch * seq, d)
        x_spec = o_spec = pl.BlockSpec((tm, d), lambda i: (i, 0))
        in_order = out_order = None
    else:
        grid = (batch // SUBLANES, seq // tl)
        in_shape, x_spec, in_order = spec(src)
        out_shape, o_spec, out_order = spec(dst)
    operands = [x.reshape(in_shape), g_in, g_out, wg, wu, wd]
    in_specs = [x_spec, _const_spec((1, d)), _const_spec((1, d)),
                _const_spec((d, f)), _const_spec((d, f)), _const_spec((f, d))]
    out_specs = [o_spec]
    out_shapes = [jax.ShapeDtypeStruct(out_shape, F32)]
    if attn is not None:
        assert src == dst
        a, w_o, g_mix = attn
        operands = [a, w_o, g_mix] + operands
        in_specs = [pl.BlockSpec((tm, a.shape[1]), lambda i: (i, 0)), _const_spec(w_o.shape),
                    _const_spec((1, d))] + in_specs
    gains, weights, norm_of, scales = proj if proj is not None else ((), (), (), ())
    if proj is not None:
        assert src == dst
        operands += list(gains) + list(weights)
        in_specs += [_const_spec((1, d))] * len(gains) + [_const_spec(w.shape) for w in weights]
        out_specs += [pl.BlockSpec((tm, w.shape[1]), lambda i: (i, 0)) for w in weights]
        out_shapes += [jax.ShapeDtypeStruct((batch * seq, w.shape[1]), BF16) for w in weights]
    outs = pl.pallas_call(
        functools.partial(_ffn_kernel, attn_proj=attn is not None, n_norm=len(gains),
                          norm_of=tuple(norm_of), scales=tuple(scales), in_order=in_order,
                          out_order=out_order),
        grid=grid,
        in_specs=in_specs,
        out_specs=out_specs,
        out_shape=out_shapes,
        compiler_params=pltpu.CompilerParams(
            dimension_semantics=("arbitrary",) * len(grid), vmem_limit_bytes=VMEM_LIMIT),
        name="ffn",
    )(*operands)
    return (outs[0].reshape(batch * seq, d), *outs[1:])


def _s5_param_kernel(lr_ref, li_ref, ldt_ref, br_ref, bi_ref, ar_ref, ai_ref, bbr_ref, bbi_ref):
    lr = lr_ref[...]
    li = li_ref[...]
    dt = jnp.exp(ldt_ref[...])
    mag = jnp.exp(lr * dt)
    ab_re = mag * jnp.cos(li * dt)
    ab_im = mag * jnp.sin(li * dt)
    den = lr * lr + li * li
    nr = ab_re - 1.0
    ni = ab_im
    f_re = (nr * lr + ni * li) / den
    f_im = (ni * lr - nr * li) / den
    b_re = br_ref[...]
    b_im = bi_ref[...]
    ar_ref[...] = ab_re
    ai_ref[...] = ab_im
    bbr_ref[...] = f_re * b_re - f_im * b_im
    bbi_ref[...] = f_re * b_im + f_im * b_re


def _s5_params(lam_re, lam_im, log_dt, b_re, b_im, c_re, c_im):
    g, n = lam_re.shape
    c = b_re.shape[-1]
    rep = lambda t: jnp.repeat(t, c, axis=0)
    b_t = lambda t: jnp.transpose(t, (0, 2, 1)).reshape(g * c, n)
    shp = jax.ShapeDtypeStruct((g * c, n), F32)
    ab_re, ab_im, bb_re, bb_im = pl.pallas_call(
        _s5_param_kernel, out_shape=(shp,) * 4, name="s5_params",
    )(rep(lam_re), rep(lam_im), rep(log_dt[:, None]), b_t(b_re), b_t(b_im))
    a_re = ab_re[::c].reshape(1, g * n)
    a_im = ab_im[::c].reshape(1, g * n)
    nb = g // GROUPS_PER_BLOCK
    eye = jnp.eye(GROUPS_PER_BLOCK, dtype=F32)

    def b_block(t):
        t = t.reshape(nb, GROUPS_PER_BLOCK, c, n)
        return jnp.einsum("jgcn,gh->jgchn", t, eye).reshape(nb, CH_BLOCK, STATES_PER_BLOCK)

    def c_block(t):
        t = t.reshape(nb, GROUPS_PER_BLOCK, c, n)
        return jnp.einsum("jgcn,gh->jhngc", t, eye).reshape(nb, STATES_PER_BLOCK, CH_BLOCK)

    b_mat = jnp.concatenate([b_block(bb_re), b_block(bb_im)], axis=2).astype(BF16)
    c_mat = jnp.concatenate([c_block(c_re), -c_block(c_im)], axis=1).astype(BF16)
    return a_re, a_im, b_mat, c_mat


def _gelu_tanh(y):
    return 0.5 * y * (1.0 + jnp.tanh(math.sqrt(2.0 / math.pi) * (y + 0.044715 * (y * y * y))))


def _s5_kernel(x_ref, gi_ref, go_ref, win_ref, bmat_ref, are_ref, aim_ref, cmat_ref, d_ref,
               wglu_ref, wout_ref, o_ref, sre_ref, sim_ref, bu_ref, sbf_ref, *, batch, steps):
    @pl.when(pl.program_id(0) == 0)
    def _():
        sre_ref[...] = jnp.zeros_like(sre_ref)
        sim_ref[...] = jnp.zeros_like(sim_ref)

    x = x_ref[...]
    hn = _rms(x, gi_ref[...]).astype(BF16)
    u = _dot(hn, win_ref[...])
    u_bf = u.astype(BF16)
    nb = bmat_ref.shape[0]
    spb = STATES_PER_BLOCK
    y_parts = []
    for j in range(nb):
        bu_ref[...] = _dot(u_bf[:, j * CH_BLOCK:(j + 1) * CH_BLOCK], bmat_ref[j])
        for c0 in range(0, spb, SCAN_COLS):
            cols = slice(j * spb + c0, j * spb + c0 + SCAN_COLS)
            a_re = jnp.broadcast_to(are_ref[:, cols], (batch, SCAN_COLS))
            a_im = jnp.broadcast_to(aim_ref[:, cols], (batch, SCAN_COLS))

            def step(t, carry, c0=c0, a_re=a_re, a_im=a_im):
                s_re, s_im = carry
                rows = pl.ds(pl.multiple_of(t * batch, batch), batch)
                n_re = a_re * s_re - a_im * s_im + bu_ref[rows, c0:c0 + SCAN_COLS]
                n_im = a_re * s_im + a_im * s_re + bu_ref[rows, spb + c0:spb + c0 + SCAN_COLS]
                sbf_ref[rows, c0:c0 + SCAN_COLS] = n_re.astype(BF16)
                sbf_ref[rows, spb + c0:spb + c0 + SCAN_COLS] = n_im.astype(BF16)
                return n_re, n_im

            s_re, s_im = lax.fori_loop(0, steps, step, (sre_ref[:, cols], sim_ref[:, cols]),
                                       unroll=True)
            sre_ref[:, cols] = s_re
            sim_ref[:, cols] = s_im
        y_parts.append(_dot(sbf_ref[...], cmat_ref[j]))
    y = jnp.concatenate(y_parts, axis=1) + d_ref[...] * u
    y = _gelu_tanh(y)
    z = y * jax.nn.sigmoid(_dot(y.astype(BF16), wglu_ref[...]))
    mix = _dot(z.astype(BF16), wout_ref[...])
    o_ref[...] = x + _rms(mix, go_ref[...])


def _s5(x, g_in, g_out, w_in, params, d_skip, w_glu, w_out, *, batch, seq):
    a_re, a_im, b_mat, c_mat = params
    d = x.shape[-1]
    rows = S5_STEPS * batch
    n_states = a_re.shape[-1]
    kern = functools.partial(_s5_kernel, batch=batch, steps=S5_STEPS)
    row_spec = pl.BlockSpec((rows, d), lambda i: (i, 0))
    return pl.pallas_call(
        kern,
        grid=(seq // S5_STEPS,),
        in_specs=[row_spec, _const_spec((1, d)), _const_spec((1, d)), _const_spec(w_in.shape),
                  _const_spec(b_mat.shape), _const_spec(a_re.shape), _const_spec(a_im.shape),
                  _const_spec(c_mat.shape), _const_spec((1, d)),
                  _const_spec(w_glu.shape), _const_spec(w_out.shape)],
        out_specs=row_spec,
        out_shape=jax.ShapeDtypeStruct(x.shape, F32),
        scratch_shapes=[pltpu.VMEM((batch, n_states), F32), pltpu.VMEM((batch, n_states), F32),
                        pltpu.VMEM((rows, 2 * STATES_PER_BLOCK), F32),
                        pltpu.VMEM((rows, 2 * STATES_PER_BLOCK), BF16)],
        compiler_params=pltpu.CompilerParams(
            dimension_semantics=("arbitrary",), vmem_limit_bytes=VMEM_LIMIT),
        name="s5_mixer",
    )(x, g_in, g_out, w_in, b_mat, a_re, a_im, c_mat, d_skip, w_glu, w_out)


def _softplus(z):
    sign_bit = jnp.uint32(0x80000000)
    neg_abs = lax.bitcast_convert_type(lax.bitcast_convert_type(z, jnp.uint32) | sign_bit, F32)
    return jnp.maximum(z, 0.0) + jnp.log(1.0 + jnp.exp(neg_abs))


def _attn_kernel(q_ref, k_ref, v_ref, o_ref, acc_ref, c_ref):
    tk = ATT_K
    n_heads = q_ref.shape[1] // SB_HEAD_DIM
    qi = pl.program_id(2)
    lane = lax.broadcasted_iota(jnp.int32, (1, LANES), 1)
    first_head = lane < SB_HEAD_DIM
    q_heads = []
    for p in range(n_heads // 2):
        q = q_ref[:, p * LANES:(p + 1) * LANES]
        zero = jnp.zeros_like(q)
        q_heads += [jnp.where(first_head, q, zero), jnp.where(first_head, zero, q)]
    r = lax.broadcasted_iota(jnp.int32, (tk, tk), 0)
    c = lax.broadcasted_iota(jnp.int32, (tk, tk), 1)
    later = jnp.where(r > c, 1.0, 0.0).astype(BF16)
    causal = c < r

    def block(kb, c_prev, diag):
        ks = pl.ds(pl.multiple_of(kb * tk, tk), tk)
        outs = []
        for h in range(n_heads):
            pair = slice(h // 2 * LANES, (h // 2 + 1) * LANES)
            k = k_ref[ks, pair]
            v = v_ref[ks, pair]
            z = lax.dot_general(q_heads[h], k, (((1,), (1,)), ((), ())), preferred_element_type=F32)
            if diag:
                z = jnp.where(causal, z, MASKED_SCORE)
            drop = _softplus(z)
            incl = _dot(drop.astype(BF16), later) + drop
            log_w = z - incl
            if c_prev is not None:
                log_w = log_w - c_prev[h]
            w = jnp.exp(log_w)
            outs.append((_dot(w.astype(BF16), v), incl[:, 0:1]))
        return outs

    @pl.when(qi == 0)
    def _():
        d = block(0, None, True)
        for h in range(n_heads):
            acc_ref[h] = d[h][0]

    @pl.when(qi > 0)
    def _():
        d = block(qi, None, True)
        p = block(qi - 1, [d[h][1] for h in range(n_heads)], False)
        for h in range(n_heads):
            acc_ref[h] = d[h][0] + p[h][0]
            c_ref[h] = d[h][1] + p[h][1]

        def cond(carry):
            kb, c_min = carry
            return jnp.logical_and(kb >= 0, c_min <= EXP_UNDERFLOW)

        def body(carry):
            kb, _ = carry
            o = block(kb, [c_ref[h] for h in range(n_heads)], False)
            c_new = [c_ref[h] + o[h][1] for h in range(n_heads)]
            for h in range(n_heads):
                acc_ref[h] += o[h][0]
                c_ref[h] = c_new[h]
            return kb - 1, jnp.min(functools.reduce(jnp.minimum, c_new))

        lax.while_loop(cond, body, (qi - 2, jnp.min(c_ref[...])))

    for p in range(n_heads // 2):
        o_ref[:, p * LANES:(p + 1) * LANES] = jnp.where(
            first_head, acc_ref[2 * p], acc_ref[2 * p + 1]).astype(o_ref.dtype)


def _attention(q, k, v, *, batch, seq):
    t, width = q.shape
    assert ATT_Q == ATT_K
    nq = seq // ATT_Q
    lanes = ATT_HEADS * SB_HEAD_DIM
    kv_spec = pl.BlockSpec((seq, lanes), lambda b, p, i: (b, p))
    q_spec = pl.BlockSpec((ATT_Q, lanes), lambda b, p, i: (b * nq + i, p))
    return pl.pallas_call(
        _attn_kernel,
        grid=(batch, width // lanes, nq),
        in_specs=[q_spec, kv_spec, kv_spec],
        out_specs=q_spec,
        out_shape=jax.ShapeDtypeStruct((t, width), BF16),
        scratch_shapes=[pltpu.VMEM((ATT_HEADS, ATT_Q, LANES), F32), pltpu.VMEM((ATT_HEADS, ATT_Q, 1), F32)],
        compiler_params=pltpu.CompilerParams(
            dimension_semantics=("arbitrary",) * 3, vmem_limit_bytes=VMEM_LIMIT),
        name="sb_attention",
    )(q, k, v)


def kernel(x, norm_g, ffn_w_gate, ffn_w_up, ffn_w_down, ssm_w_in, ssm_lam_re, ssm_lam_im, ssm_log_dt, ssm_b_re, ssm_b_im, ssm_c_re, ssm_c_im, ssm_d, ssm_w_glu, ssm_w_out, kv_norm_g, w_k, w_v, sb_w_q, sb_w_o):
    batch, seq, d = x.shape
    depth = norm_g.shape[0]
    n_a = ssm_w_in.shape[0]
    bf = lambda t: t.astype(BF16)
    gain = lambda layer, i: norm_g[layer, i][None, :]
    q_scale = 1.0 / math.sqrt(SB_HEAD_DIM)

    h = x.reshape(batch * seq, d)
    order = "bl"
    k_shared = v_shared = None
    for layer in range(depth):
        is_ssm = layer < n_a
        attn = proj = None
        want = "lb" if is_ssm else "bl"
        if not is_ssm:
            b = layer - n_a
            if k_shared is None:
                proj = ([gain(layer, 2), kv_norm_g[None, :]], [bf(sb_w_q[b]), bf(w_k), bf(w_v)],
                        (0, 1, 1), (q_scale, 1.0, 1.0))
            else:
                proj = ([gain(layer, 2)], [bf(sb_w_q[b])], (0,), (q_scale,))
        h, *projected = _ffn(h, gain(layer, 0), gain(layer, 1), bf(ffn_w_gate[layer, 0]),
                             bf(ffn_w_up[layer, 0]), bf(ffn_w_down[layer, 0]), batch=batch, seq=seq,
                             src=order, dst=want, proj=proj)
        order = want
        if is_ssm:
            a = layer
            params = _s5_params(ssm_lam_re[a], ssm_lam_im[a], ssm_log_dt[a], ssm_b_re[a], ssm_b_im[a],
                                ssm_c_re[a], ssm_c_im[a])
            h = _s5(h, gain(layer, 2), gain(layer, 3), bf(ssm_w_in[a]), params,
                    ssm_d[a].reshape(1, d), bf(ssm_w_glu[a]), bf(ssm_w_out[a]), batch=batch, seq=seq)
        else:
            if len(projected) == 3:
                q, k_shared, v_shared = projected
            else:
                (q,) = projected
            attn = (_attention(q, k_shared, v_shared, batch=batch, seq=seq), bf(sb_w_o[b]), gain(layer, 3))
        nxt = "lb" if (layer + 1 < n_a) else "bl"
        (h,) = _ffn(h, gain(layer, 4), gain(layer, 5), bf(ffn_w_gate[layer, 1]), bf(ffn_w_up[layer, 1]),
                    bf(ffn_w_down[layer, 1]), batch=batch, seq=seq, src=order, dst=nxt, attn=attn)
        order = nxt
    return h.reshape(batch, seq, d)
```

```python
import functools
import math

import jax
import jax.numpy as jnp
from jax import lax
from jax.experimental import pallas as pl
from jax.experimental.pallas import tpu as pltpu

F32 = jnp.float32
BF16 = jnp.bfloat16

NORM_EPS = 1e-6
SSM_GROUP = 16
SSM_STATE = 64
SB_HEAD_DIM = 64
LANES = 128
SUBLANES = 8
CH_BLOCK = 256
GROUPS_PER_BLOCK = CH_BLOCK // SSM_GROUP
STATES_PER_BLOCK = GROUPS_PER_BLOCK * SSM_STATE
SCAN_COLS = 512
VMEM_LIMIT = 56 * 1024 * 1024

FFN_ROWS = 1024
FFN_PARTS = 4
S5_STEPS = 64
ATT_Q = 256
ATT_K = 256
ATT_HEADS = 8
EXP_UNDERFLOW = 104.5
MASKED_SCORE = -1e30


def _rms(x, g):
    ms = jnp.mean(x * x, axis=-1, keepdims=True)
    return x * lax.rsqrt(ms + NORM_EPS) * g


def _dot(a, b):
    return jnp.dot(a, b, preferred_element_type=F32)


def _const_spec(shape):
    nd = len(shape)
    return pl.BlockSpec(shape, lambda *_: (0,) * nd, pipeline_mode=pl.Buffered(1))


def _ffn_kernel(*refs, attn_proj, n_norm, norm_of, scales, in_order, out_order):
    refs = list(refs)
    if attn_proj:
        a_ref, wo_ref, gm_ref = refs[:3]
        refs = refs[3:]
    x_ref, gi_ref, go_ref, wg_ref, wu_ref, wd_ref = refs[:6]
    n_proj = len(norm_of)
    pg_refs = refs[6:6 + n_norm]
    pw_refs = refs[6 + n_norm:6 + n_norm + n_proj]
    o_ref = refs[6 + n_norm + n_proj]
    po_refs = refs[7 + n_norm + n_proj:]
    d = x_ref.shape[-1]
    n_rows = x_ref.size // d
    part = n_rows // FFN_PARTS
    t_part = part // SUBLANES
    for i in range(FFN_PARTS):
        rows = slice(i * part, (i + 1) * part)
        steps = slice(i * t_part, (i + 1) * t_part)
        if in_order is None:
            x = x_ref[rows, :]
        elif in_order == "bt":
            x = x_ref[:, steps, :].reshape(part, d)
        else:
            x = pltpu.einshape("tbd->btd", x_ref[steps]).reshape(part, d)
        if attn_proj:
            x = x + _rms(_dot(a_ref[rows, :], wo_ref[...]), gm_ref[...])
        h = _rms(x, gi_ref[...]).astype(BF16)
        g = _dot(h, wg_ref[...])
        u = _dot(h, wu_ref[...])
        a = (g * jax.nn.sigmoid(g) * u).astype(BF16)
        y = _dot(a, wd_ref[...])
        out = x + 0.5 * _rms(y, go_ref[...])
        if out_order is None:
            o_ref[rows, :] = out
        elif out_order == "bt":
            o_ref[:, steps, :] = out.reshape(SUBLANES, t_part, d)
        else:
            o_ref[steps] = pltpu.einshape("btd->tbd", out.reshape(SUBLANES, t_part, d))
        hs = [_rms(out, pg[...]).astype(BF16) for pg in pg_refs]
        for pw_ref, po_ref, gi, sc in zip(pw_refs, po_refs, norm_of, scales):
            p = _dot(hs[gi], pw_ref[...])
            if sc != 1.0:
                p = p * sc
            po_ref[rows, :] = p.astype(po_ref.dtype)


def _ffn(x, g_in, g_out, wg, wu, wd, *, batch, seq, src, dst, attn=None, proj=None):
    d = x.shape[-1]
    f = wg.shape[-1]
    tm = FFN_ROWS
    tl = tm // SUBLANES

    def spec(order):
        if order == "bl":
            return (batch, seq, d), pl.BlockSpec((SUBLANES, tl, d), lambda g, l: (g, l, 0)), "bt"
        return (seq, batch, d), pl.BlockSpec((tl, SUBLANES, d), lambda g, l: (l, g, 0)), "tb"

    if src == dst:
        n = (batch * seq) // tm
        grid = (n,)
        in_shape = out_shape = (batch * seq, d)
        x_spec = o_spec = pl.BlockSpec((tm, d), lambda i: (i, 0))
        in_order = out_order = None
    else:
        grid = (batch // SUBLANES, seq // tl)
        in_shape, x_spec, in_order = spec(src)
        out_shape, o_spec, out_order = spec(dst)
    operands = [x.reshape(in_shape), g_in, g_out, wg, wu, wd]
    in_specs = [x_spec, _const_spec((1, d)), _const_spec((1, d)),
                _const_spec((d, f)), _const_spec((d, f)), _const_spec((f, d))]
    out_specs = [o_spec]
    out_shapes = [jax.ShapeDtypeStruct(out_shape, F32)]
    if attn is not None:
        assert src == dst
        a, w_o, g_mix = attn
        operands = [a, w_o, g_mix] + operands
        in_specs = [pl.BlockSpec((tm, a.shape[1]), lambda i: (i, 0)), _const_spec(w_o.shape),
                    _const_spec((1, d))] + in_specs
    gains, weights, norm_of, scales = proj if proj is not None else ((), (), (), ())
    if proj is not None:
        assert src == dst
        operands += list(gains) + list(weights)
        in_specs += [_const_spec((1, d))] * len(gains) + [_const_spec(w.shape) for w in weights]
        out_specs += [pl.BlockSpec((tm, w.shape[1]), lambda i: (i, 0)) for w in weights]
        out_shapes += [jax.ShapeDtypeStruct((batch * seq, w.shape[1]), BF16) for w in weights]
    outs = pl.pallas_call(
        functools.partial(_ffn_kernel, attn_proj=attn is not None, n_norm=len(gains),
                          norm_of=tuple(norm_of), scales=tuple(scales), in_order=in_order,
                          out_order=out_order),
        grid=grid,
        in_specs=in_specs,
        out_specs=out_specs,
        out_shape=out_shapes,
        compiler_params=pltpu.CompilerParams(
            dimension_semantics=("arbitrary",) * len(grid), vmem_limit_bytes=VMEM_LIMIT),
        name="ffn",
    )(*operands)
    return (outs[0].reshape(batch * seq, d), *outs[1:])


def _s5_param_kernel(lr_ref, li_ref, ldt_ref, br_ref, bi_ref, ar_ref, ai_ref, bbr_ref, bbi_ref):
    lr = lr_ref[...]
    li = li_ref[...]
    dt = jnp.exp(ldt_ref[...])
    mag = jnp.exp(lr * dt)
    ab_re = mag * jnp.cos(li * dt)
    ab_im = mag * jnp.sin(li * dt)
    den = lr * lr + li * li
    nr = ab_re - 1.0
    ni = ab_im
    f_re = (nr * lr + ni * li) / den
    f_im = (ni * lr - nr * li) / den
    b_re = br_ref[...]
    b_im = bi_ref[...]
    ar_ref[...] = ab_re
    ai_ref[...] = ab_im
    bbr_ref[...] = f_re * b_re - f_im * b_im
    bbi_ref[...] = f_re * b_im + f_im * b_re


def _s5_params(lam_re, lam_im, log_dt, b_re, b_im, c_re, c_im):
    g, n = lam_re.shape
    c = b_re.shape[-1]
    rep = lambda t: jnp.repeat(t, c, axis=0)
    b_t = lambda t: jnp.transpose(t, (0, 2, 1)).reshape(g * c, n)
    shp = jax.ShapeDtypeStruct((g * c, n), F32)
    ab_re, ab_im, bb_re, bb_im = pl.pallas_call(
        _s5_param_kernel, out_shape=(shp,) * 4, name="s5_params",
    )(rep(lam_re), rep(lam_im), rep(log_dt[:, None]), b_t(b_re), b_t(b_im))
    a_re = ab_re[::c].reshape(1, g * n)
    a_im = ab_im[::c].reshape(1, g * n)
    nb = g // GROUPS_PER_BLOCK
    eye = jnp.eye(GROUPS_PER_BLOCK, dtype=F32)

    def b_block(t):
        t = t.reshape(nb, GROUPS_PER_BLOCK, c, n)
        return jnp.einsum("jgcn,gh->jgchn", t, eye).reshape(nb, CH_BLOCK, STATES_PER_BLOCK)

    def c_block(t):
        t = t.reshape(nb, GROUPS_PER_BLOCK, c, n)
        return jnp.einsum("jgcn,gh->jhngc", t, eye).reshape(nb, STATES_PER_BLOCK, CH_BLOCK)

    b_mat = jnp.concatenate([b_block(bb_re), b_block(bb_im)], axis=2).astype(BF16)
    c_mat = jnp.concatenate([c_block(c_re), -c_block(c_im)], axis=1).astype(BF16)
    return a_re, a_im, b_mat, c_mat


def _gelu_tanh(y):
    return 0.5 * y * (1.0 + jnp.tanh(math.sqrt(2.0 / math.pi) * (y + 0.044715 * (y * y * y))))


def _s5_kernel(x_ref, gi_ref, go_ref, win_ref, bmat_ref, are_ref, aim_ref, cmat_ref, d_ref,
               wglu_ref, wout_ref, o_ref, sre_ref, sim_ref, bu_ref, sbf_ref, *, batch, steps):
    @pl.when(pl.program_id(0) == 0)
    def _():
        sre_ref[...] = jnp.zeros_like(sre_ref)
        sim_ref[...] = jnp.zeros_like(sim_ref)

    x = x_ref[...]
    hn = _rms(x, gi_ref[...]).astype(BF16)
    u = _dot(hn, win_ref[...])
    u_bf = u.astype(BF16)
    nb = bmat_ref.shape[0]
    spb = STATES_PER_BLOCK
    y_parts = []
    for j in range(nb):
        bu_ref[...] = _dot(u_bf[:, j * CH_BLOCK:(j + 1) * CH_BLOCK], bmat_ref[j])
        for c0 in range(0, spb, SCAN_COLS):
            cols = slice(j * spb + c0, j * spb + c0 + SCAN_COLS)
            a_re = jnp.broadcast_to(are_ref[:, cols], (batch, SCAN_COLS))
            a_im = jnp.broadcast_to(aim_ref[:, cols], (batch, SCAN_COLS))

            def step(t, carry, c0=c0, a_re=a_re, a_im=a_im):
                s_re, s_im = carry
                rows = pl.ds(pl.multiple_of(t * batch, batch), batch)
                n_re = a_re * s_re - a_im * s_im + bu_ref[rows, c0:c0 + SCAN_COLS]
                n_im = a_re * s_im + a_im * s_re + bu_ref[rows, spb + c0:spb + c0 + SCAN_COLS]
                sbf_ref[rows, c0:c0 + SCAN_COLS] = n_re.astype(BF16)
                sbf_ref[rows, spb + c0:spb + c0 + SCAN_COLS] = n_im.astype(BF16)
                return n_re, n_im

            s_re, s_im = lax.fori_loop(0, steps, step, (sre_ref[:, cols], sim_ref[:, cols]),
                                       unroll=True)
            sre_ref[:, cols] = s_re
            sim_ref[:, cols] = s_im
        y_parts.append(_dot(sbf_ref[...], cmat_ref[j]))
    y = jnp.concatenate(y_parts, axis=1) + d_ref[...] * u
    y = _gelu_tanh(y)
    z = y * jax.nn.sigmoid(_dot(y.astype(BF16), wglu_ref[...]))
    mix = _dot(z.astype(BF16), wout_ref[...])
    o_ref[...] = x + _rms(mix, go_ref[...])


def _s5(x, g_in, g_out, w_in, params, d_skip, w_glu, w_out, *, batch, seq):
    a_re, a_im, b_mat, c_mat = params
    d = x.shape[-1]
    rows = S5_STEPS * batch
    n_states = a_re.shape[-1]
    kern = functools.partial(_s5_kernel, batch=batch, steps=S5_STEPS)
    row_spec = pl.BlockSpec((rows, d), lambda i: (i, 0))
    return pl.pallas_call(
        kern,
        grid=(seq // S5_STEPS,),
        in_specs=[row_spec, _const_spec((1, d)), _const_spec((1, d)), _const_spec(w_in.shape),
                  _const_spec(b_mat.shape), _const_spec(a_re.shape), _const_spec(a_im.shape),
                  _const_spec(c_mat.shape), _const_spec((1, d)),
                  _const_spec(w_glu.shape), _const_spec(w_out.shape)],
        out_specs=row_spec,
        out_shape=jax.ShapeDtypeStruct(x.shape, F32),
        scratch_shapes=[pltpu.VMEM((batch, n_states), F32), pltpu.VMEM((batch, n_states), F32),
                        pltpu.VMEM((rows, 2 * STATES_PER_BLOCK), F32),
                        pltpu.VMEM((rows, 2 * STATES_PER_BLOCK), BF16)],
        compiler_params=pltpu.CompilerParams(
            dimension_semantics=("arbitrary",), vmem_limit_bytes=VMEM_LIMIT),
        name="s5_mixer",
    )(x, g_in, g_out, w_in, b_mat, a_re, a_im, c_mat, d_skip, w_glu, w_out)


def _softplus(z):
    sign_bit = jnp.uint32(0x80000000)
    neg_abs = lax.bitcast_convert_type(lax.bitcast_convert_type(z, jnp.uint32) | sign_bit, F32)
    return jnp.maximum(z, 0.0) + jnp.log(1.0 + jnp.exp(neg_abs))


def _attn_kernel(q_ref, k_ref, v_ref, o_ref, acc_ref, c_ref):
    tk = ATT_K
    n_heads = q_ref.shape[1] // SB_HEAD_DIM
    qi = pl.program_id(2)
    lane = lax.broadcasted_iota(jnp.int32, (1, LANES), 1)
    first_head = lane < SB_HEAD_DIM
    q_heads = []
    for p in range(n_heads // 2):
        q = q_ref[:, p * LANES:(p + 1) * LANES]
        zero = jnp.zeros_like(q)
        q_heads += [jnp.where(first_head, q, zero), jnp.where(first_head, zero, q)]
    r = lax.broadcasted_iota(jnp.int32, (tk, tk), 0)
    c = lax.broadcasted_iota(jnp.int32, (tk, tk), 1)
    later = jnp.where(r > c, 1.0, 0.0).astype(BF16)
    causal = c < r

    def block(kb, c_prev, diag):
        ks = pl.ds(pl.multiple_of(kb * tk, tk), tk)
        outs = []
        for h in range(n_heads):
            pair = slice(h // 2 * LANES, (h // 2 + 1) * LANES)
            k = k_ref[ks, pair]
            v = v_ref[ks, pair]
            z = lax.dot_general(q_heads[h], k, (((1,), (1,)), ((), ())), preferred_element_type=F32)
            if diag:
                z = jnp.where(causal, z, MASKED_SCORE)
            drop = _softplus(z)
            incl = _dot(drop.astype(BF16), later) + drop
            log_w = z - incl
            if c_prev is not None:
                log_w = log_w - c_prev[h]
            w = jnp.exp(log_w)
            outs.append((_dot(w.astype(BF16), v), incl[:, 0:1]))
        return outs

    @pl.when(qi == 0)
    def _():
        d = block(0, None, True)
        for h in range(n_heads):
            acc_ref[h] = d[h][0]

    @pl.when(qi > 0)
    def _():
        d = block(qi, None, True)
        p = block(qi - 1, [d[h][1] for h in range(n_heads)], False)
        for h in range(n_heads):
            acc_ref[h] = d[h][0] + p[h][0]
            c_ref[h] = d[h][1] + p[h][1]

        def cond(carry):
            kb, c_min = carry
            return jnp.logical_and(kb >= 0, c_min <= EXP_UNDERFLOW)

        def body(carry):
            kb, _ = carry
            o = block(kb, [c_ref[h] for h in range(n_heads)], False)
            c_new = [c_ref[h] + o[h][1] for h in range(n_heads)]
            for h in range(n_heads):
                acc_ref[h] += o[h][0]
                c_ref[h] = c_new[h]
            return kb - 1, jnp.min(functools.reduce(jnp.minimum, c_new))

        lax.while_loop(cond, body, (qi - 2, jnp.min(c_ref[...])))

    for p in range(n_heads // 2):
        o_ref[:, p * LANES:(p + 1) * LANES] = jnp.where(
            first_head, acc_ref[2 * p], acc_ref[2 * p + 1]).astype(o_ref.dtype)


def _attention(q, k, v, *, batch, seq):
    t, width = q.shape
    assert ATT_Q == ATT_K
    nq = seq // ATT_Q
    lanes = ATT_HEADS * SB_HEAD_DIM
    kv_spec = pl.BlockSpec((seq, lanes), lambda b, p, i: (b, p))
    q_spec = pl.BlockSpec((ATT_Q, lanes), lambda b, p, i: (b * nq + i, p))
    return pl.pallas_call(
        _attn_kernel,
        grid=(batch, width // lanes, nq),
        in_specs=[q_spec, kv_spec, kv_spec],
        out_specs=q_spec,
        out_shape=jax.ShapeDtypeStruct((t, width), BF16),
        scratch_shapes=[pltpu.VMEM((ATT_HEADS, ATT_Q, LANES), F32), pltpu.VMEM((ATT_HEADS, ATT_Q, 1), F32)],
        compiler_params=pltpu.CompilerParams(
            dimension_semantics=("arbitrary",) * 3, vmem_limit_bytes=VMEM_LIMIT),
        name="sb_attention",
    )(q, k, v)


def kernel(x, norm_g, ffn_w_gate, ffn_w_up, ffn_w_down, ssm_w_in, ssm_lam_re, ssm_lam_im, ssm_log_dt, ssm_b_re, ssm_b_im, ssm_c_re, ssm_c_im, ssm_d, ssm_w_glu, ssm_w_out, kv_norm_g, w_k, w_v, sb_w_q, sb_w_o):
    batch, seq, d = x.shape
    depth = norm_g.shape[0]
    n_a = ssm_w_in.shape[0]
    bf = lambda t: t.astype(BF16)
    gain = lambda layer, i: norm_g[layer, i][None, :]
    q_scale = 1.0 / math.sqrt(SB_HEAD_DIM)

    h = x.reshape(batch * seq, d)
    order = "bl"
    k_shared = v_shared = None
    for layer in range(depth):
        is_ssm = layer < n_a
        attn = proj = None
        want = "lb" if is_ssm else "bl"
        if not is_ssm:
            b = layer - n_a
            if k_shared is None:
                proj = ([gain(layer, 2), kv_norm_g[None, :]], [bf(sb_w_q[b]), bf(w_k), bf(w_v)],
                        (0, 1, 1), (q_scale, 1.0, 1.0))
            else:
                proj = ([gain(layer, 2)], [bf(sb_w_q[b])], (0,), (q_scale,))
        h, *projected = _ffn(h, gain(layer, 0), gain(layer, 1), bf(ffn_w_gate[layer, 0]),
                             bf(ffn_w_up[layer, 0]), bf(ffn_w_down[layer, 0]), batch=batch, seq=seq,
                             src=order, dst=want, proj=proj)
        order = want
        if is_ssm:
            a = layer
            params = _s5_params(ssm_lam_re[a], ssm_lam_im[a], ssm_log_dt[a], ssm_b_re[a], ssm_b_im[a],
                                ssm_c_re[a], ssm_c_im[a])
            h = _s5(h, gain(layer, 2), gain(layer, 3), bf(ssm_w_in[a]), params,
                    ssm_d[a].reshape(1, d), bf(ssm_w_glu[a]), bf(ssm_w_out[a]), batch=batch, seq=seq)
        else:
            if len(projected) == 3:
                q, k_shared, v_shared = projected
            else:
                (q,) = projected
            attn = (_attention(q, k_shared, v_shared, batch=batch, seq=seq), bf(sb_w_o[b]), gain(layer, 3))
        nxt = "lb" if (layer + 1 < n_a) else "bl"
        (h,) = _ffn(h, gain(layer, 4), gain(layer, 5), bf(ffn_w_gate[layer, 1]), bf(ffn_w_up[layer, 1]),
                    bf(ffn_w_down[layer, 1]), batch=batch, seq=seq, src=order, dst=nxt, attn=attn)
        order = nxt
    return h.reshape(batch, seq, d)
```
